```python
import jax, jax.numpy as jnp
from jax import lax
import numpy as np

D_MODEL = 1024
BATCH = 16
SEQ = 2048
DEPTH = 2

N_META = 16
HEAD_DIM = 64
N_Q_HEADS = D_MODEL // HEAD_DIM
N_KV_HEADS = N_Q_HEADS // 4
GQA_GROUP = N_Q_HEADS // N_KV_HEADS
WINDOW = 128
BLOCK = 128
POOL_WINDOWS = (2, 4, 8, 16)
N_POOL_GROUPS = len(POOL_WINDOWS)
POOL_GROUP_DIM = D_MODEL // N_POOL_GROUPS
D_FF = ((8 * D_MODEL // 3 + 255) // 256) * 256
N_MIXERS = 2
N_ATTN_LAYERS = (DEPTH + 1) // N_MIXERS
N_POOL_LAYERS = DEPTH // N_MIXERS
QKV_DIM = (N_Q_HEADS + 2 * N_KV_HEADS) * HEAD_DIM
RMS_EPS = 1e-6
NEG_INF = -1e30

kernel_name = 'hybrid_window_gqa_multiscale_pool_macaron'


def rms_norm(x, g):
    xf = x.astype(jnp.float32)
    y = xf * lax.rsqrt(jnp.mean(xf * xf, axis=-1, keepdims=True) + RMS_EPS)
    return (y * g.astype(jnp.float32)).astype(x.dtype)


def swiglu(x, w_gu, w_down):
    g, u = jnp.split(x @ w_gu, 2, axis=-1)
    return (jax.nn.silu(g) * u) @ w_down


def alibi_slopes(n):
    return jnp.exp2(-8.0 * jnp.arange(1, n + 1, dtype=jnp.float32) / n)


def windowed_gqa_attention(h, w_qkv, q_gain, k_gain, sink, w_o):
    B, L, _ = h.shape
    S = L - N_META
    nb = S // BLOCK
    f32 = jnp.float32
    q, k, v = jnp.split(h @ w_qkv, [N_Q_HEADS * HEAD_DIM, (N_Q_HEADS + N_KV_HEADS) * HEAD_DIM], axis=-1)
    q = rms_norm(q.reshape(B, L, N_KV_HEADS, GQA_GROUP, HEAD_DIM), q_gain) * (HEAD_DIM ** -0.5)
    k = rms_norm(k.reshape(B, L, N_KV_HEADS, HEAD_DIM), k_gain)
    v = v.reshape(B, L, N_KV_HEADS, HEAD_DIM)
    qm, qr = q[:, :N_META], q[:, N_META:]
    km, kr = k[:, :N_META], k[:, N_META:]
    vm, vr = v[:, :N_META], v[:, N_META:]
    slopes = alibi_slopes(N_Q_HEADS).reshape(N_KV_HEADS, GQA_GROUP)
    sink = sink.astype(f32).reshape(N_KV_HEADS, GQA_GROUP)

    qb = qr.reshape(B, nb, BLOCK, N_KV_HEADS, GQA_GROUP, HEAD_DIM)

    def band(t):
        tp = jnp.pad(t, ((0, 0), (BLOCK, BLOCK), (0, 0), (0, 0))).reshape(B, nb + 2, BLOCK, N_KV_HEADS, HEAD_DIM)
        return jnp.concatenate([tp[:, :-2], tp[:, 1:-1], tp[:, 2:]], axis=2)

    kb, vb = band(kr), band(vr)
    s_band = jnp.einsum('bnqkgd,bnckd->bnkgqc', qb, kb).astype(f32)
    s_meta = jnp.einsum('bnqkgd,bmkd->bnkgqm', qb, km).astype(f32)
    a = jnp.arange(BLOCK)
    c = jnp.arange(3 * BLOCK)
    blk = jnp.arange(nb)
    rel = c[None, :] - BLOCK - a[:, None]
    key_idx = blk[:, None] * BLOCK - BLOCK + c[None, :]
    valid = (jnp.abs(rel) <= WINDOW)[None] & ((key_idx >= 0) & (key_idx < S))[:, None, :]
    band_bias = -slopes[:, :, None, None] * jnp.abs(rel).astype(f32)
    s_band = jnp.where(valid[None, :, None, None], s_band + band_bias[None, None], NEG_INF)
    t_glob = N_META + blk[:, None] * BLOCK + a[None, :]
    meta_dist = (t_glob[:, :, None] - jnp.arange(N_META)[None, None, :]).astype(f32)
    s_meta = s_meta - slopes[:, :, None, None] * meta_dist[:, None, None]
    sink_col = jnp.broadcast_to(sink[:, :, None, None], s_band.shape[:-1] + (1,))
    p = jax.nn.softmax(jnp.concatenate([s_meta, s_band, sink_col], axis=-1), axis=-1)[..., :-1].astype(h.dtype)
    o_r = (jnp.einsum('bnkgqm,bmkd->bnqkgd', p[..., :N_META], vm)
           + jnp.einsum('bnkgqc,bnckd->bnqkgd', p[..., N_META:], vb))
    o_r = o_r.reshape(B, S, N_Q_HEADS * HEAD_DIM)

    k_mq = jnp.concatenate([km, kr[:, :BLOCK]], axis=1)
    v_mq = jnp.concatenate([vm, vr[:, :BLOCK]], axis=1)
    s_mq = jnp.einsum('bpkgd,bskd->bkgps', qm, k_mq).astype(f32)
    dist = jnp.abs(jnp.arange(N_META)[:, None] - jnp.arange(N_META + BLOCK)[None, :])
    s_mq = jnp.where(dist <= WINDOW, s_mq - slopes[:, :, None, None] * dist.astype(f32), NEG_INF)
    sink_mq = jnp.broadcast_to(sink[:, :, None, None], s_mq.shape[:-1] + (1,))
    p_mq = jax.nn.softmax(jnp.concatenate([s_mq, sink_mq], axis=-1), axis=-1)[..., :-1].astype(h.dtype)
    o_m = jnp.einsum('bkgps,bskd->bpkgd', p_mq, v_mq).reshape(B, N_META, N_Q_HEADS * HEAD_DIM)

    return jnp.concatenate([o_m, o_r], axis=1) @ w_o


def multiscale_pool_mixer(h, w_in, w_group, scale, w_out):
    B, L, _ = h.shape
    f32 = jnp.float32
    uf = (h @ w_in).astype(f32).reshape(B, L, N_POOL_GROUPS, POOL_GROUP_DIM)
    cs = jnp.pad(jnp.cumsum(uf, axis=1), ((0, 0), (1, 0), (0, 0), (0, 0)))
    t = jnp.arange(L)
    half = jnp.array(POOL_WINDOWS, dtype=jnp.int32) // 2
    lo = jnp.clip(t[:, None] - half[None, :], 0, L)
    hi = jnp.clip(t[:, None] + half[None, :], 0, L)
    g_idx = jnp.arange(N_POOL_GROUPS)[None, :]
    win_sum = cs[:, hi, g_idx] - cs[:, lo, g_idx]
    mean = win_sum / (hi - lo).astype(f32)[None, :, :, None]
    pooled = (mean - uf).astype(h.dtype)
    mixed = jnp.einsum('blgc,gcd->blgd', pooled, w_group).reshape(B, L, D_MODEL)
    return (mixed * scale) @ w_out


def setup_inputs(seed: int = 0) -> dict:
    key = jax.random.key(seed)
    ks = jax.random.split(key, 16)
    nrm = jax.random.normal
    f32 = jnp.float32
    return {
        'x': nrm(ks[0], (BATCH, SEQ, D_MODEL), f32),
        'meta_tokens': nrm(ks[1], (N_META, D_MODEL), f32),
        'ffn_norm': 1.0 + 0.02 * nrm(ks[2], (DEPTH, 2, D_MODEL), f32),
        'w_gate_up': nrm(ks[3], (DEPTH, 2, D_MODEL, 2 * D_FF), f32) * D_MODEL ** -0.5,
        'w_down': nrm(ks[4], (DEPTH, 2, D_FF, D_MODEL), f32) * D_FF ** -0.5,
        'mixer_norm': 1.0 + 0.02 * nrm(ks[5], (DEPTH, D_MODEL), f32),
        'w_qkv': nrm(ks[6], (N_ATTN_LAYERS, D_MODEL, QKV_DIM), f32) * D_MODEL ** -0.5,
        'q_norm': 1.0 + 0.02 * nrm(ks[7], (N_ATTN_LAYERS, HEAD_DIM), f32),
        'k_norm': 1.0 + 0.02 * nrm(ks[8], (N_ATTN_LAYERS, HEAD_DIM), f32),
        'sink_logit': nrm(ks[9], (N_ATTN_LAYERS, N_Q_HEADS), f32),
        'w_o': nrm(ks[10], (N_ATTN_LAYERS, N_Q_HEADS * HEAD_DIM, D_MODEL), f32) * (N_Q_HEADS * HEAD_DIM) ** -0.5,
        'w_pool_in': nrm(ks[11], (N_POOL_LAYERS, D_MODEL, D_MODEL), f32) * D_MODEL ** -0.5,
        'w_pool_group': nrm(ks[12], (N_POOL_LAYERS, N_POOL_GROUPS, POOL_GROUP_DIM, POOL_GROUP_DIM), f32) * POOL_GROUP_DIM ** -0.5,
        'pool_scale': 1.0 + 0.02 * nrm(ks[13], (N_POOL_LAYERS, D_MODEL), f32),
        'w_pool_out': nrm(ks[14], (N_POOL_LAYERS, D_MODEL, D_MODEL), f32) * D_MODEL ** -0.5,
    }


def reference(x, meta_tokens, ffn_norm, w_gate_up, w_down, mixer_norm, w_qkv, q_norm, k_norm,
              sink_logit, w_o, w_pool_in, w_pool_group, pool_scale, w_pool_out):
    B = x.shape[0]
    meta = jnp.broadcast_to(meta_tokens[None].astype(x.dtype), (B, N_META, D_MODEL))
    h = jnp.concatenate([meta, x], axis=1)
    for i in range(DEPTH):
        h = h + 0.5 * swiglu(rms_norm(h, ffn_norm[i, 0]), w_gate_up[i, 0], w_down[i, 0])
        hn = rms_norm(h, mixer_norm[i])
        j = i // N_MIXERS
        if i % N_MIXERS == 0:
            h = h + windowed_gqa_attention(hn, w_qkv[j], q_norm[j], k_norm[j], sink_logit[j], w_o[j])
        else:
            h = h + multiscale_pool_mixer(hn, w_pool_in[j], w_pool_group[j], pool_scale[j], w_pool_out[j])
        h = h + 0.5 * swiglu(rms_norm(h, ffn_norm[i, 1]), w_gate_up[i, 1], w_down[i, 1])
    return h[:, N_META:]
```

```python
import functools

import numpy as np
import jax
import jax.numpy as jnp
from jax import lax
from jax.experimental import pallas as pl
from jax.experimental.pallas import tpu as pltpu

F32 = jnp.float32
BF16 = jnp.bfloat16

N_META = 16
HEAD_DIM = 64
GQA_GROUP = 4
WINDOW = 128
BLOCK = 128
POOL_WINDOWS = (2, 4, 8, 16)
RMS_EPS = 1e-6
NEG_INF = -1e30

LANES = 128
MXU_DIM = 256
HEADS_PER_LANE_GROUP = LANES // HEAD_DIM
BAND = 3 * BLOCK
POOL_HALO = 16

TOKEN_TILE = 512
FF_CHUNK = 256
VMEM_LIMIT = 56 * 1024 * 1024


def _params(n_axes, vmem=VMEM_LIMIT):
    return pltpu.CompilerParams(dimension_semantics=("arbitrary",) * n_axes, vmem_limit_bytes=vmem)


def _resident(shape):
    nd = len(shape)
    return pl.BlockSpec(shape, lambda *_: (0,) * nd, pipeline_mode=pl.Buffered(1))


def _rows(tm, d):
    return pl.BlockSpec((tm, d), lambda i: (i, 0))


def _rms_norm(x, gain):
    return x * lax.rsqrt(jnp.mean(x * x, axis=-1, keepdims=True) + RMS_EPS) * gain


def _dot(a, b):
    return jnp.dot(a, b, preferred_element_type=F32)


def _dot_nt(a, b):
    return lax.dot_general(a, b, (((1,), (1,)), ((), ())), preferred_element_type=F32)


def _ffn_kernel(x_ref, gain_ref, wgu_ref, wd_ref, o_ref, act_ref, *, d_ff, chunk):
    x = x_ref[...]
    xn = _rms_norm(x, gain_ref[...]).astype(BF16)
    for c in range(d_ff // chunk):
        gate = _dot(xn, wgu_ref[:, c * chunk:(c + 1) * chunk])
        up = _dot(xn, wgu_ref[:, d_ff + c * chunk:d_ff + (c + 1) * chunk])
        act_ref[:, c * chunk:(c + 1) * chunk] = (jax.nn.silu(gate) * up).astype(BF16)
    o_ref[...] = x + 0.5 * _dot(act_ref[...], wd_ref[...])


def _ffn(h, gain, wgu, wd):
    t, d = h.shape
    d_ff = wd.shape[0]
    tm = min(TOKEN_TILE, t)
    return pl.pallas_call(
        functools.partial(_ffn_kernel, d_ff=d_ff, chunk=FF_CHUNK),
        out_shape=jax.ShapeDtypeStruct((t, d), h.dtype),
        grid=(t // tm,),
        in_specs=[_rows(tm, d), _resident((1, d)), _resident(wgu.shape), _resident(wd.shape)],
        out_specs=_rows(tm, d),
        scratch_shapes=[pltpu.VMEM((tm, d_ff), BF16)],
        compiler_params=_params(1),
        name="ffn",
    )(h, gain, wgu, wd)


def _head_mean_square(t, seg_ref):
    sq = t * t
    hi = sq.astype(BF16)
    lo = (sq - hi.astype(F32)).astype(BF16)
    seg = seg_ref[...]
    cols = []
    for c in range(t.shape[1] // MXU_DIM):
        sl = slice(c * MXU_DIM, (c + 1) * MXU_DIM)
        cols.append(_dot(hi[:, sl], seg) + _dot(lo[:, sl], seg))
    return cols[0] if len(cols) == 1 else jnp.concatenate(cols, axis=1)


def _qkv_kernel(h_ref, gain_ref, w_ref, qgain_ref, kgain_ref, seg_ref, q_ref, k_ref, v_ref, *, dq, dkv):
    hn = _rms_norm(h_ref[...], gain_ref[...]).astype(BF16)
    qkv = _dot(hn, w_ref[...])
    q = qkv[:, :dq]
    k = qkv[:, dq:dq + dkv]
    q = q * lax.rsqrt(_head_mean_square(q, seg_ref) + RMS_EPS) * qgain_ref[...] * (HEAD_DIM ** -0.5)
    k = k * lax.rsqrt(_head_mean_square(k, seg_ref) + RMS_EPS) * kgain_ref[...]
    q_ref[...] = q.astype(q_ref.dtype)
    k_ref[...] = k.astype(k_ref.dtype)
    v_ref[...] = qkv[:, dq + dkv:].astype(v_ref.dtype)


def _qkv(h, gain, w, qgain, kgain, seg):
    t, d = h.shape
    dq, dkv = qgain.shape[1], kgain.shape[1]
    tm = min(TOKEN_TILE, t)
    return pl.pallas_call(
        functools.partial(_qkv_kernel, dq=dq, dkv=dkv),
        out_shape=[jax.ShapeDtypeStruct((t, dq), BF16), jax.ShapeDtypeStruct((t, dkv), BF16),
                   jax.ShapeDtypeStruct((t, dkv), BF16)],
        grid=(t // tm,),
        in_specs=[_rows(tm, d), _resident((1, d)), _resident(w.shape), _resident((1, dq)), _resident((1, dkv)),
                  _resident(seg.shape)],
        out_specs=[_rows(tm, dq), _rows(tm, dkv), _rows(tm, dkv)],
        compiler_params=_params(1),
        name="qkv",
    )(h, gain, w, qgain, kgain, seg)


def _alibi_slope(head, n_heads):
    return 2.0 ** (-8.0 * (head + 1) / n_heads)


def _attend(q, keys, vals, dist, valid, sink_ref, o_ref):
    rows = q.shape[0]
    n_heads = q.shape[1] // HEAD_DIM
    low = lax.broadcasted_iota(jnp.int32, (rows, LANES), 1) < HEAD_DIM
    zero = jnp.zeros((rows, LANES), q.dtype)
    for j in range(keys.shape[1] // LANES):
        kg = keys[:, j * LANES:(j + 1) * LANES]
        vg = vals[:, j * LANES:(j + 1) * LANES]
        groups = [q[:, (j * GQA_GROUP + g) * LANES:(j * GQA_GROUP + g + 1) * LANES] for g in range(GQA_GROUP)]
        stacked = jnp.concatenate([jnp.where(low, qg, zero) for qg in groups]
                                  + [jnp.where(low, zero, qg) for qg in groups], axis=0)
        scores = _dot_nt(stacked, kg)
        probs = []
        for e in range(HEADS_PER_LANE_GROUP):
            for g in range(GQA_GROUP):
                head = (HEADS_PER_LANE_GROUP * j + e) * GQA_GROUP + g
                r0 = (e * GQA_GROUP + g) * rows
                s = scores[r0:r0 + rows] - _alibi_slope(head, n_heads) * dist
                s = jnp.where(valid, s, NEG_INF)
                sink = sink_ref[head]
                m = jnp.maximum(jnp.max(s, axis=-1, keepdims=True), sink)
                ex = jnp.exp(s - m)
                denom = jnp.sum(ex, axis=-1, keepdims=True) + jnp.exp(sink - m)
                probs.append((ex * (1.0 / denom)).astype(vg.dtype))
        out = _dot(jnp.concatenate(probs, axis=0), vg)
        for g in range(GQA_GROUP):
            og = jnp.where(low, out[g * rows:(g + 1) * rows], out[(GQA_GROUP + g) * rows:(GQA_GROUP + g + 1) * rows])
            lanes = slice((j * GQA_GROUP + g) * LANES, (j * GQA_GROUP + g + 1) * LANES)
            o_ref[:, lanes] = og.astype(o_ref.dtype)


def _attn_real_kernel(sink_ref, q_ref, k_ref, v_ref, km_ref, vm_ref, o_ref, *, seq):
    n = pl.program_id(1)
    start = pl.multiple_of(jnp.clip((n - 1) * BLOCK, 0, seq - BAND), BLOCK)
    keys = jnp.concatenate([k_ref[pl.ds(start, BAND), :], km_ref[...]], axis=0)
    vals = jnp.concatenate([v_ref[pl.ds(start, BAND), :], vm_ref[...]], axis=0)
    nk = BAND + N_META
    qpos = n * BLOCK + lax.broadcasted_iota(jnp.int32, (BLOCK, nk), 0)
    col = lax.broadcasted_iota(jnp.int32, (BLOCK, nk), 1)
    is_meta = col >= BAND
    band_dist = jnp.abs(start + col - qpos)
    meta_dist = N_META + qpos - (col - BAND)
    dist = jnp.where(is_meta, meta_dist, band_dist).astype(F32)
    valid = is_meta | (band_dist <= WINDOW)
    _attend(q_ref[...], keys, vals, dist, valid, sink_ref, o_ref)


def _attn_meta_kernel(sink_ref, q_ref, k_ref, v_ref, km_ref, vm_ref, o_ref):
    keys = jnp.concatenate([k_ref[...], km_ref[...]], axis=0)
    vals = jnp.concatenate([v_ref[...], vm_ref[...]], axis=0)
    nk = BLOCK + N_META
    qpos = lax.broadcasted_iota(jnp.int32, (N_META, nk), 0)
    col = lax.broadcasted_iota(jnp.int32, (N_META, nk), 1)
    kpos = jnp.where(col < BLOCK, N_META + col, col - BLOCK)
    idist = jnp.abs(qpos - kpos)
    _attend(q_ref[...], keys, vals, idist.astype(F32), idist <= WINDOW, sink_ref, o_ref)


def _attention(sink, q, k, v, qm, km, vm):
    b, s, dq = q.shape
    dkv = k.shape[2]
    smem = pl.BlockSpec(memory_space=pltpu.SMEM)
    meta_kv = pl.BlockSpec((None, N_META, dkv), lambda i, *_: (i, 0, 0))
    o_real = pl.pallas_call(
        functools.partial(_attn_real_kernel, seq=s),
        out_shape=jax.ShapeDtypeStruct((b, s, dq), BF16),
        grid=(b, s // BLOCK),
        in_specs=[smem,
                  pl.BlockSpec((None, BLOCK, dq), lambda i, n: (i, n, 0)),
                  pl.BlockSpec((None, s, dkv), lambda i, n: (i, 0, 0)),
                  pl.BlockSpec((None, s, dkv), lambda i, n: (i, 0, 0)),
                  meta_kv, meta_kv],
        out_specs=pl.BlockSpec((None, BLOCK, dq), lambda i, n: (i, n, 0)),
        compiler_params=_params(2),
        name="attn_real",
    )(sink, q, k, v, km, vm)
    first_block = pl.BlockSpec((None, BLOCK, dkv), lambda i: (i, 0, 0))
    o_meta = pl.pallas_call(
        _attn_meta_kernel,
        out_shape=jax.ShapeDtypeStruct((b, N_META, dq), BF16),
        grid=(b,),
        in_specs=[smem, pl.BlockSpec((None, N_META, dq), lambda i: (i, 0, 0)), first_block, first_block,
                  meta_kv, meta_kv],
        out_specs=pl.BlockSpec((None, N_META, dq), lambda i: (i, 0, 0)),
        compiler_params=_params(1),
        name="attn_meta",
    )(sink, qm, k, v, km, vm)
    return o_real, o_meta


def _oproj_kernel(h_ref, o_ref, w_ref, out_ref):
    out_ref[...] = h_ref[...] + _dot(o_ref[...], w_ref[...])


def _oproj(h, o, w):
    t, d = h.shape
    tm = min(TOKEN_TILE, t)
    return pl.pallas_call(
        _oproj_kernel,
        out_shape=jax.ShapeDtypeStruct((t, d), h.dtype),
        grid=(t // tm,),
        in_specs=[_rows(tm, d), _rows(tm, o.shape[1]), _resident(w.shape)],
        out_specs=_rows(tm, d),
        compiler_params=_params(1),
        name="oproj",
    )(h, o, w)


def _pool_kernel(h_ref, prev_ref, next_ref, meta_ref, gain_ref, win_ref, wgrp_ref, scale_ref, wout_ref, o_ref,
                 hn_ref, u_ref, *, tm, total_len):
    i = pl.program_id(1)
    last = pl.num_programs(1) - 1
    gain = gain_ref[...]
    before = jnp.where(i == 0, meta_ref[...], prev_ref[...])
    hn_ref[0:POOL_HALO, :] = _rms_norm(before, gain).astype(BF16)
    hn_ref[POOL_HALO:POOL_HALO + tm, :] = _rms_norm(h_ref[...], gain).astype(BF16)
    after = _rms_norm(next_ref[...], gain)
    hn_ref[POOL_HALO + tm:, :] = jnp.where(i == last, 0.0, after).astype(BF16)
    u_ref[...] = _dot(hn_ref[...], win_ref[...])

    n_groups = len(POOL_WINDOWS)
    gdim = u_ref.shape[1] // n_groups
    pos = N_META + i * tm + lax.broadcasted_iota(jnp.int32, (tm, 1), 0)
    mixed = []
    for g, window in enumerate(POOL_WINDOWS):
        half = window // 2
        cols = slice(g * gdim, (g + 1) * gdim)
        total = u_ref[POOL_HALO - half:POOL_HALO - half + tm, cols]
        for off in range(1 - half, half):
            total = total + u_ref[POOL_HALO + off:POOL_HALO + off + tm, cols]
        count = (half + jnp.minimum(half, total_len - pos)).astype(F32)
        pooled = total / count - u_ref[POOL_HALO:POOL_HALO + tm, cols]
        mixed.append(_dot(pooled.astype(BF16), wgrp_ref[g]))
    y = (jnp.concatenate(mixed, axis=1) * scale_ref[...]).astype(BF16)
    o_ref[...] = h_ref[...] + _dot(y, wout_ref[...])


def _pool_mixer(h, h_meta, gain, w_in, w_grp, scale, w_out):
    b, s, d = h.shape
    tm = TOKEN_TILE
    per_tile = tm // POOL_HALO
    n_halo_blocks = s // POOL_HALO
    return pl.pallas_call(
        functools.partial(_pool_kernel, tm=tm, total_len=N_META + s),
        out_shape=jax.ShapeDtypeStruct((b, s, d), h.dtype),
        grid=(b, s // tm),
        in_specs=[pl.BlockSpec((None, tm, d), lambda bi, i: (bi, i, 0)),
                  pl.BlockSpec((None, POOL_HALO, d), lambda bi, i: (bi, jnp.maximum(i * per_tile - 1, 0), 0)),
                  pl.BlockSpec((None, POOL_HALO, d),
                               lambda bi, i: (bi, jnp.minimum((i + 1) * per_tile, n_halo_blocks - 1), 0)),
                  pl.BlockSpec((None, N_META, d), lambda bi, i: (bi, 0, 0)),
                  _resident((1, d)), _resident(w_in.shape), _resident(w_grp.shape), _resident((1, d)),
                  _resident(w_out.shape)],
        out_specs=pl.BlockSpec((None, tm, d), lambda bi, i: (bi, i, 0)),
        scratch_shapes=[pltpu.VMEM((tm + 2 * POOL_HALO, d), BF16), pltpu.VMEM((tm + 2 * POOL_HALO, d), F32)],
        compiler_params=_params(2),
        name="pool_mixer",
    )(h, h, h, h_meta, gain, w_in, w_grp, scale, w_out)


def _head_permutation(n_heads):
    n_kv = n_heads // GQA_GROUP
    cols = []
    for j in range(n_kv // HEADS_PER_LANE_GROUP):
        for g in range(GQA_GROUP):
            for e in range(HEADS_PER_LANE_GROUP):
                head = (HEADS_PER_LANE_GROUP * j + e) * GQA_GROUP + g
                cols.extend(range(head * HEAD_DIM, (head + 1) * HEAD_DIM))
    return np.asarray(cols, dtype=np.int32)


def _segment_mean_matrix():
    seg = np.kron(np.eye(MXU_DIM // HEAD_DIM), np.ones((HEAD_DIM, HEAD_DIM))) / HEAD_DIM
    return jnp.asarray(seg, dtype=BF16)


def kernel(x, meta_tokens, ffn_norm, w_gate_up, w_down, mixer_norm, w_qkv, q_norm, k_norm, sink_logit, w_o,
           w_pool_in, w_pool_group, pool_scale, w_pool_out):
    b, s, d = x.shape
    depth = ffn_norm.shape[0]
    n_heads = sink_logit.shape[1]
    dq = n_heads * HEAD_DIM
    dkv = dq // GQA_GROUP
    assert depth == 2 and s % TOKEN_TILE == 0 and d == dq
    perm = _head_permutation(n_heads)
    seg = _segment_mean_matrix()

    hr = x.reshape(b * s, d)
    hm = jnp.broadcast_to(meta_tokens[None].astype(x.dtype), (b, N_META, d)).reshape(b * N_META, d)

    def ffn(h, layer, which):
        return _ffn(h, ffn_norm[layer, which][None], w_gate_up[layer, which].astype(BF16),
                    w_down[layer, which].astype(BF16))

    hr, hm = ffn(hr, 0, 0), ffn(hm, 0, 0)
    w_qkv_p = jnp.concatenate([w_qkv[0][:, :dq][:, perm], w_qkv[0][:, dq:]], axis=1).astype(BF16)
    qgain = jnp.tile(q_norm[0], n_heads)[None]
    kgain = jnp.tile(k_norm[0], dkv // HEAD_DIM)[None]
    gain = mixer_norm[0][None]
    q, k, v = _qkv(hr, gain, w_qkv_p, qgain, kgain, seg)
    qm, km, vm = _qkv(hm, gain, w_qkv_p, qgain, kgain, seg)
    o_real, o_meta = _attention(sink_logit[0], q.reshape(b, s, dq), k.reshape(b, s, dkv), v.reshape(b, s, dkv),
                                qm.reshape(b, N_META, dq), km.reshape(b, N_META, dkv), vm.reshape(b, N_META, dkv))
    w_o_p = w_o[0][perm, :].astype(BF16)
    hr = _oproj(hr, o_real.reshape(b * s, dq), w_o_p)
    hm = _oproj(hm, o_meta.reshape(b * N_META, dq), w_o_p)
    hr, hm = ffn(hr, 0, 1), ffn(hm, 0, 1)

    hr, hm = ffn(hr, 1, 0), ffn(hm, 1, 0)
    hr = _pool_mixer(hr.reshape(b, s, d), hm.reshape(b, N_META, d), mixer_norm[1][None], w_pool_in[0].astype(BF16),
                     w_pool_group[0].astype(BF16), pool_scale[0][None], w_pool_out[0].astype(BF16))
    hr = ffn(hr.reshape(b * s, d), 1, 1)
    return hr.reshape(b, s, d)
```

```python
import functools

import numpy as np
import jax
import jax.numpy as jnp
from jax import lax
from jax.experimental import pallas as pl
from jax.experimental.pallas import tpu as pltpu

F32 = jnp.float32
BF16 = jnp.bfloat16

N_META = 16
HEAD_DIM = 64
GQA_GROUP = 4
WINDOW = 128
BLOCK = 128
POOL_WINDOWS = (2, 4, 8, 16)
RMS_EPS = 1e-6
NEG_INF = -1e30
LOG2E = 1.4426950408889634

LANES = 128
MXU_DIM = 256
HEADS_PER_LANE_GROUP = LANES // HEAD_DIM
BAND = 3 * BLOCK
POOL_HALO = 16

TOKEN_TILE = 512
FF_CHUNK = 256
VMEM_LIMIT = 56 * 1024 * 1024


def _params(n_axes, vmem=VMEM_LIMIT):
    return pltpu.CompilerParams(dimension_semantics=("arbitrary",) * n_axes, vmem_limit_bytes=vmem)


def _resident(shape):
    nd = len(shape)
    return pl.BlockSpec(shape, lambda *_: (0,) * nd, pipeline_mode=pl.Buffered(1))


def _rows(tm, d):
    return pl.BlockSpec((tm, d), lambda i: (i, 0))


def _rms_norm(x, gain):
    return x * lax.rsqrt(jnp.mean(x * x, axis=-1, keepdims=True) + RMS_EPS) * gain


def _dot(a, b):
    return jnp.dot(a, b, preferred_element_type=F32)


def _dot_nt(a, b):
    return lax.dot_general(a, b, (((1,), (1,)), ((), ())), preferred_element_type=F32)


def _ffn_kernel(x_ref, gain_ref, wgu_ref, wd_ref, o_ref, act_ref, *, d_ff, chunk):
    x = x_ref[...]
    xn = _rms_norm(x, gain_ref[...]).astype(BF16)
    for c in range(d_ff // chunk):
        gate = _dot(xn, wgu_ref[:, c * chunk:(c + 1) * chunk])
        up = _dot(xn, wgu_ref[:, d_ff + c * chunk:d_ff + (c + 1) * chunk])
        act_ref[:, c * chunk:(c + 1) * chunk] = (jax.nn.silu(gate) * up).astype(BF16)
    o_ref[...] = x + 0.5 * _dot(act_ref[...], wd_ref[...])


def _ffn(h, gain, wgu, wd):
    t, d = h.shape
    d_ff = wd.shape[0]
    tm = min(TOKEN_TILE, t)
    return pl.pallas_call(
        functools.partial(_ffn_kernel, d_ff=d_ff, chunk=FF_CHUNK),
        out_shape=jax.ShapeDtypeStruct((t, d), h.dtype),
        grid=(t // tm,),
        in_specs=[_rows(tm, d), _resident((1, d)), _resident(wgu.shape), _resident(wd.shape)],
        out_specs=_rows(tm, d),
        scratch_shapes=[pltpu.VMEM((tm, d_ff), BF16)],
        compiler_params=_params(1),
        name="ffn",
    )(h, gain, wgu, wd)


def _head_mean_square(t, seg_ref):
    sq = t * t
    hi = sq.astype(BF16)
    lo = (sq - hi.astype(F32)).astype(BF16)
    seg = seg_ref[...]
    cols = []
    for c in range(t.shape[1] // MXU_DIM):
        sl = slice(c * MXU_DIM, (c + 1) * MXU_DIM)
        cols.append(_dot(hi[:, sl], seg) + _dot(lo[:, sl], seg))
    return cols[0] if len(cols) == 1 else jnp.concatenate(cols, axis=1)


def _qkv_kernel(h_ref, gain_ref, w_ref, qgain_ref, kgain_ref, seg_ref, q_ref, k_ref, vt_ref, *, dq, dkv):
    hn = _rms_norm(h_ref[...], gain_ref[...]).astype(BF16)
    qkv = _dot(hn, w_ref[...])
    q = qkv[:, :dq]
    k = qkv[:, dq:dq + dkv]
    v = qkv[:, dq + dkv:]
    q = q * lax.rsqrt(_head_mean_square(q, seg_ref) + RMS_EPS) * qgain_ref[...] * (HEAD_DIM ** -0.5 * LOG2E)
    k = k * lax.rsqrt(_head_mean_square(k, seg_ref) + RMS_EPS) * kgain_ref[...]
    q_ref[...] = q.astype(q_ref.dtype)
    k_ref[...] = k.astype(k_ref.dtype)
    for t in range(vt_ref.shape[0]):
        vt_ref[t] = v[t * BLOCK:(t + 1) * BLOCK, :].T.astype(vt_ref.dtype)


def _qkv(h, gain, w, qgain, kgain, seg):
    t, d = h.shape
    dq, dkv = qgain.shape[1], kgain.shape[1]
    tm = min(TOKEN_TILE, t)
    return pl.pallas_call(
        functools.partial(_qkv_kernel, dq=dq, dkv=dkv),
        out_shape=[jax.ShapeDtypeStruct((t, dq), BF16), jax.ShapeDtypeStruct((t, dkv), BF16),
                   jax.ShapeDtypeStruct((t // BLOCK, dkv, BLOCK), BF16)],
        grid=(t // tm,),
        in_specs=[_rows(tm, d), _resident((1, d)), _resident(w.shape), _resident((1, dq)), _resident((1, dkv)),
                  _resident(seg.shape)],
        out_specs=[_rows(tm, dq), _rows(tm, dkv), pl.BlockSpec((tm // BLOCK, dkv, BLOCK), lambda i: (i, 0, 0))],
        compiler_params=_params(1),
        name="qkv",
    )(h, gain, w, qgain, kgain, seg)


def _slope2(head, n_heads):
    return 2.0 ** (-8.0 * (head + 1) / n_heads) * LOG2E


def _head_index(j, e, g):
    return (HEADS_PER_LANE_GROUP * j + e) * GQA_GROUP + g


def _stack_queries(q, j):
    rows = q.shape[0]
    low = lax.broadcasted_iota(jnp.int32, (rows, LANES), 1) < HEAD_DIM
    zero = jnp.zeros((rows, LANES), q.dtype)
    groups = [q[:, (j * GQA_GROUP + g) * LANES:(j * GQA_GROUP + g + 1) * LANES] for g in range(GQA_GROUP)]
    return jnp.concatenate([jnp.where(low, qg, zero) for qg in groups]
                           + [jnp.where(low, zero, qg) for qg in groups], axis=0)


def _attn_real_kernel(sink_ref, q_ref, k_ref, vt_ref, km_ref, vmt_ref, o_ref,
                      bias_l_ref, bias_c_ref, bias_r_ref, bias_m_ref, vall_ref, p_ref):
    first = (pl.program_id(0) == 0) & (pl.program_id(1) == 0)
    n = pl.program_id(1)
    last = pl.num_programs(1) - 1
    n_heads = q_ref.shape[1] // HEAD_DIM

    @pl.when(first)
    def _init_tables():
        key = lax.broadcasted_iota(jnp.int32, (BLOCK, BLOCK), 0)
        qry = lax.broadcasted_iota(jnp.int32, (BLOCK, BLOCK), 1)
        masked = jnp.full((BLOCK, BLOCK), NEG_INF, F32)
        meta = lax.broadcasted_iota(jnp.int32, (N_META, BLOCK), 0)
        mqry = lax.broadcasted_iota(jnp.int32, (N_META, BLOCK), 1)
        for h in range(n_heads):
            slope = _slope2(h, n_heads)
            for ref, delta in ((bias_l_ref, key - BLOCK - qry), (bias_c_ref, key - qry), (bias_r_ref, key + BLOCK - qry)):
                dist = jnp.abs(delta)
                table = jnp.where(dist <= WINDOW, -slope * dist.astype(F32), NEG_INF)
                if ref is bias_c_ref:
                    ref[h] = table
                else:
                    ref[0, h] = table
                    ref[1, h] = masked
            bias_m_ref[h] = -slope * (N_META + mqry - meta).astype(F32)
        vall_ref[...] = jnp.zeros_like(vall_ref)
        p_ref[...] = jnp.zeros_like(p_ref)

    left = jnp.maximum(n - 1, 0)
    right = jnp.minimum(n + 1, last)
    no_left = (n == 0).astype(jnp.int32)
    no_right = (n == last).astype(jnp.int32)
    blocks = (left, n, right)
    keys = jnp.concatenate([k_ref[pl.ds(pl.multiple_of(blk * BLOCK, BLOCK), BLOCK), :] for blk in blocks]
                           + [km_ref[...]], axis=0)
    for i, blk in enumerate(blocks):
        vall_ref[:, i * BLOCK:(i + 1) * BLOCK] = vt_ref[blk]
    vall_ref[:, BAND:BAND + N_META] = vmt_ref[...]
    q = q_ref[...]
    block_shift = (n * BLOCK).astype(F32)
    upper = lax.broadcasted_iota(jnp.int32, (LANES, BLOCK), 0) < HEAD_DIM

    for j in range(keys.shape[1] // LANES):
        st = _dot_nt(keys[:, j * LANES:(j + 1) * LANES], _stack_queries(q, j))
        inv = []
        for e in range(HEADS_PER_LANE_GROUP):
            for g in range(GQA_GROUP):
                head = _head_index(j, e, g)
                cols = slice((e * GQA_GROUP + g) * BLOCK, (e * GQA_GROUP + g + 1) * BLOCK)
                s_l = st[0:BLOCK, cols] + bias_l_ref[no_left, head]
                s_c = st[BLOCK:2 * BLOCK, cols] + bias_c_ref[head]
                s_r = st[2 * BLOCK:BAND, cols] + bias_r_ref[no_right, head]
                s_m = st[BAND:BAND + N_META, cols] + (bias_m_ref[head] - _slope2(head, n_heads) * block_shift)
                sink = sink_ref[head] * LOG2E
                m = jnp.max(jnp.maximum(jnp.maximum(s_l, s_c), s_r), axis=0, keepdims=True)
                m = jnp.maximum(jnp.maximum(m, jnp.max(s_m, axis=0, keepdims=True)), sink)
                e_l, e_c, e_r, e_m = (jnp.exp2(s - m) for s in (s_l, s_c, s_r, s_m))
                denom = (jnp.sum(e_l + e_c + e_r, axis=0, keepdims=True) + jnp.sum(e_m, axis=0, keepdims=True)
                         + jnp.exp2(sink - m))
                inv.append(1.0 / denom)
                p_ref[j, 0:BLOCK, cols] = e_l.astype(p_ref.dtype)
                p_ref[j, BLOCK:2 * BLOCK, cols] = e_c.astype(p_ref.dtype)
                p_ref[j, 2 * BLOCK:BAND, cols] = e_r.astype(p_ref.dtype)
                p_ref[j, BAND:BAND + N_META, cols] = e_m.astype(p_ref.dtype)
        out_t = _dot(vall_ref[j * LANES:(j + 1) * LANES, :], p_ref[j])
        for g in range(GQA_GROUP):
            c0, c1 = g * BLOCK, (GQA_GROUP + g) * BLOCK
            og_t = jnp.where(upper, out_t[:, c0:c0 + BLOCK] * inv[g], out_t[:, c1:c1 + BLOCK] * inv[GQA_GROUP + g])
            lanes = slice((j * GQA_GROUP + g) * LANES, (j * GQA_GROUP + g + 1) * LANES)
            o_ref[:, lanes] = og_t.T.astype(o_ref.dtype)


def _attn_meta_kernel(sink_ref, q_ref, k_ref, vt_ref, km_ref, vm_ref, o_ref):
    rows = q_ref.shape[0]
    n_heads = q_ref.shape[1] // HEAD_DIM
    keys = jnp.concatenate([k_ref[...], km_ref[...]], axis=0)
    vals = jnp.concatenate([vt_ref[0].astype(F32).T.astype(BF16), vm_ref[...]], axis=0)
    nk = BLOCK + N_META
    qpos = lax.broadcasted_iota(jnp.int32, (rows, nk), 0)
    col = lax.broadcasted_iota(jnp.int32, (rows, nk), 1)
    idist = jnp.abs(qpos - jnp.where(col < BLOCK, N_META + col, col - BLOCK))
    valid = idist <= WINDOW
    dist = idist.astype(F32)
    low = lax.broadcasted_iota(jnp.int32, (rows, LANES), 1) < HEAD_DIM
    q = q_ref[...]
    for j in range(keys.shape[1] // LANES):
        scores = _dot_nt(_stack_queries(q, j), keys[:, j * LANES:(j + 1) * LANES])
        probs = []
        for e in range(HEADS_PER_LANE_GROUP):
            for g in range(GQA_GROUP):
                head = _head_index(j, e, g)
                r0 = (e * GQA_GROUP + g) * rows
                s = jnp.where(valid, scores[r0:r0 + rows] - _slope2(head, n_heads) * dist, NEG_INF)
                sink = sink_ref[head] * LOG2E
                m = jnp.maximum(jnp.max(s, axis=-1, keepdims=True), sink)
                ex = jnp.exp2(s - m)
                denom = jnp.sum(ex, axis=-1, keepdims=True) + jnp.exp2(sink - m)
                probs.append((ex * (1.0 / denom)).astype(BF16))
        out = _dot(jnp.concatenate(probs, axis=0), vals[:, j * LANES:(j + 1) * LANES])
        for g in range(GQA_GROUP):
            og = jnp.where(low, out[g * rows:(g + 1) * rows], out[(GQA_GROUP + g) * rows:(GQA_GROUP + g + 1) * rows])
            lanes = slice((j * GQA_GROUP + g) * LANES, (j * GQA_GROUP + g + 1) * LANES)
            o_ref[:, lanes] = og.astype(o_ref.dtype)


def _attention(sink, q, k, vt, qm, km, vm, vmt):
    b, s, dq = q.shape
    dkv = k.shape[2]
    nb = s // BLOCK
    n_heads = dq // HEAD_DIM
    n_pairs = dkv // LANES
    padded_keys = 2 * MXU_DIM
    smem = pl.BlockSpec(memory_space=pltpu.SMEM)
    o_real = pl.pallas_call(
        _attn_real_kernel,
        out_shape=jax.ShapeDtypeStruct((b, s, dq), BF16),
        grid=(b, nb),
        in_specs=[smem,
                  pl.BlockSpec((None, BLOCK, dq), lambda i, n: (i, n, 0)),
                  pl.BlockSpec((None, s, dkv), lambda i, n: (i, 0, 0)),
                  pl.BlockSpec((nb, dkv, BLOCK), lambda i, n: (i, 0, 0)),
                  pl.BlockSpec((None, N_META, dkv), lambda i, n: (i, 0, 0)),
                  pl.BlockSpec((None, dkv, N_META), lambda i, n: (i, 0, 0))],
        out_specs=pl.BlockSpec((None, BLOCK, dq), lambda i, n: (i, n, 0)),
        scratch_shapes=[pltpu.VMEM((2, n_heads, BLOCK, BLOCK), F32), pltpu.VMEM((n_heads, BLOCK, BLOCK), F32),
                        pltpu.VMEM((2, n_heads, BLOCK, BLOCK), F32), pltpu.VMEM((n_heads, N_META, BLOCK), F32),
                        pltpu.VMEM((dkv, padded_keys), BF16),
                        pltpu.VMEM((n_pairs, padded_keys, 2 * GQA_GROUP * BLOCK), BF16)],
        compiler_params=_params(2),
        name="attn_real",
    )(sink, q, k, vt, km, vmt)
    o_meta = pl.pallas_call(
        _attn_meta_kernel,
        out_shape=jax.ShapeDtypeStruct((b, N_META, dq), BF16),
        grid=(b,),
        in_specs=[smem, pl.BlockSpec((None, N_META, dq), lambda i: (i, 0, 0)),
                  pl.BlockSpec((None, BLOCK, dkv), lambda i: (i, 0, 0)),
                  pl.BlockSpec((1, dkv, BLOCK), lambda i: (i * nb, 0, 0)),
                  pl.BlockSpec((None, N_META, dkv), lambda i: (i, 0, 0)),
                  pl.BlockSpec((None, N_META, dkv), lambda i: (i, 0, 0))],
        out_specs=pl.BlockSpec((None, N_META, dq), lambda i: (i, 0, 0)),
        compiler_params=_params(1),
        name="attn_meta",
    )(sink, qm, k, vt, km, vm)
    return o_real, o_meta


def _oproj_kernel(h_ref, o_ref, w_ref, out_ref):
    out_ref[...] = h_ref[...] + _dot(o_ref[...], w_ref[...])


def _oproj(h, o, w):
    t, d = h.shape
    tm = min(TOKEN_TILE, t)
    return pl.pallas_call(
        _oproj_kernel,
        out_shape=jax.ShapeDtypeStruct((t, d), h.dtype),
        grid=(t // tm,),
        in_specs=[_rows(tm, d), _rows(tm, o.shape[1]), _resident(w.shape)],
        out_specs=_rows(tm, d),
        compiler_params=_params(1),
        name="oproj",
    )(h, o, w)


def _pool_kernel(h_ref, prev_ref, next_ref, meta_ref, gain_ref, win_ref, wgrp_ref, scale_ref, wout_ref, o_ref,
                 hn_ref, u_ref, *, tm, total_len):
    i = pl.program_id(1)
    last = pl.num_programs(1) - 1
    gain = gain_ref[...]
    before = jnp.where(i == 0, meta_ref[...], prev_ref[...])
    hn_ref[0:POOL_HALO, :] = _rms_norm(before, gain).astype(BF16)
    hn_ref[POOL_HALO:POOL_HALO + tm, :] = _rms_norm(h_ref[...], gain).astype(BF16)
    after = _rms_norm(next_ref[...], gain)
    hn_ref[POOL_HALO + tm:, :] = jnp.where(i == last, 0.0, after).astype(BF16)
    u_ref[...] = _dot(hn_ref[...], win_ref[...])

    n_groups = len(POOL_WINDOWS)
    gdim = u_ref.shape[1] // n_groups
    pos = N_META + i * tm + lax.broadcasted_iota(jnp.int32, (tm, 1), 0)
    mixed = []
    for g, window in enumerate(POOL_WINDOWS):
        half = window // 2
        cols = slice(g * gdim, (g + 1) * gdim)
        total = u_ref[POOL_HALO - half:POOL_HALO - half + tm, cols]
        for off in range(1 - half, half):
            total = total + u_ref[POOL_HALO + off:POOL_HALO + off + tm, cols]
        count = (half + jnp.minimum(half, total_len - pos)).astype(F32)
        pooled = total / count - u_ref[POOL_HALO:POOL_HALO + tm, cols]
        mixed.append(_dot(pooled.astype(BF16), wgrp_ref[g]))
    y = (jnp.concatenate(mixed, axis=1) * scale_ref[...]).astype(BF16)
    o_ref[...] = h_ref[...] + _dot(y, wout_ref[...])


def _pool_mixer(h, h_meta, gain, w_in, w_grp, scale, w_out):
    b, s, d = h.shape
    tm = TOKEN_TILE
    per_tile = tm // POOL_HALO
    n_halo_blocks = s // POOL_HALO
    return pl.pallas_call(
        functools.partial(_pool_kernel, tm=tm, total_len=N_META + s),
        out_shape=jax.ShapeDtypeStruct((b, s, d), h.dtype),
        grid=(b, s // tm),
        in_specs=[pl.BlockSpec((None, tm, d), lambda bi, i: (bi, i, 0)),
                  pl.BlockSpec((None, POOL_HALO, d), lambda bi, i: (bi, jnp.maximum(i * per_tile - 1, 0), 0)),
                  pl.BlockSpec((None, POOL_HALO, d),
                               lambda bi, i: (bi, jnp.minimum((i + 1) * per_tile, n_halo_blocks - 1), 0)),
                  pl.BlockSpec((None, N_META, d), lambda bi, i: (bi, 0, 0)),
                  _resident((1, d)), _resident(w_in.shape), _resident(w_grp.shape), _resident((1, d)),
                  _resident(w_out.shape)],
        out_specs=pl.BlockSpec((None, tm, d), lambda bi, i: (bi, i, 0)),
        scratch_shapes=[pltpu.VMEM((tm + 2 * POOL_HALO, d), BF16), pltpu.VMEM((tm + 2 * POOL_HALO, d), F32)],
        compiler_params=_params(2),
        name="pool_mixer",
    )(h, h, h, h_meta, gain, w_in, w_grp, scale, w_out)


def _head_permutation(n_heads):
    n_kv = n_heads // GQA_GROUP
    cols = []
    for j in range(n_kv // HEADS_PER_LANE_GROUP):
        for g in range(GQA_GROUP):
            for e in range(HEADS_PER_LANE_GROUP):
                head = (HEADS_PER_LANE_GROUP * j + e) * GQA_GROUP + g
                cols.extend(range(head * HEAD_DIM, (head + 1) * HEAD_DIM))
    return np.asarray(cols, dtype=np.int32)


def _segment_mean_matrix():
    seg = np.kron(np.eye(MXU_DIM // HEAD_DIM), np.ones((HEAD_DIM, HEAD_DIM))) / HEAD_DIM
    return jnp.asarray(seg, dtype=BF16)


def kernel(x, meta_tokens, ffn_norm, w_gate_up, w_down, mixer_norm, w_qkv, q_norm, k_norm, sink_logit, w_o,
           w_pool_in, w_pool_group, pool_scale, w_pool_out):
    b, s, d = x.shape
    depth = ffn_norm.shape[0]
    n_heads = sink_logit.shape[1]
    dq = n_heads * HEAD_DIM
    dkv = dq // GQA_GROUP
    assert depth == 2 and s % TOKEN_TILE == 0 and d == dq
    perm = _head_permutation(n_heads)
    seg = _segment_mean_matrix()

    hr = x.reshape(b * s, d)
    hm = jnp.broadcast_to(meta_tokens[None].astype(x.dtype), (b, N_META, d)).reshape(b * N_META, d)

    def ffn(h, layer, which):
        return _ffn(h, ffn_norm[layer, which][None], w_gate_up[layer, which].astype(BF16),
                    w_down[layer, which].astype(BF16))

    hr, hm = ffn(hr, 0, 0), ffn(hm, 0, 0)
    w_qkv_p = jnp.concatenate([w_qkv[0][:, :dq][:, perm], w_qkv[0][:, dq:]], axis=1).astype(BF16)
    qgain = jnp.tile(q_norm[0], n_heads)[None]
    kgain = jnp.tile(k_norm[0], dkv // HEAD_DIM)[None]
    gain = mixer_norm[0][None]
    q, k, vt = _qkv(hr, gain, w_qkv_p, qgain, kgain, seg)
    qm, km, vmt = _qkv(hm, gain, w_qkv_p, qgain, kgain, seg)
    vm = vmt.transpose(0, 2, 1).reshape(b, N_META, dkv)
    o_real, o_meta = _attention(sink_logit[0], q.reshape(b, s, dq), k.reshape(b, s, dkv), vt,
                                qm.reshape(b, N_META, dq), km.reshape(b, N_META, dkv), vm, vm.transpose(0, 2, 1))
    w_o_p = w_o[0][perm, :].astype(BF16)
    hr = _oproj(hr, o_real.reshape(b * s, dq), w_o_p)
    hm = _oproj(hm, o_meta.reshape(b * N_META, dq), w_o_p)
    hr, hm = ffn(hr, 0, 1), ffn(hm, 0, 1)

    hr, hm = ffn(hr, 1, 0), ffn(hm, 1, 0)
    hr = _pool_mixer(hr.reshape(b, s, d), hm.reshape(b, N_META, d), mixer_norm[1][None], w_pool_in[0].astype(BF16),
                     w_pool_group[0].astype(BF16), pool_scale[0][None], w_pool_out[0].astype(BF16))
    hr = ffn(hr.reshape(b * s, d), 1, 1)
    return hr.reshape(b, s, d)
```

```python
import functools

import numpy as np
import jax
import jax.numpy as jnp
from jax import lax
from jax.experimental import pallas as pl
from jax.experimental.pallas import tpu as pltpu

F32 = jnp.float32
BF16 = jnp.bfloat16

N_META = 16
HEAD_DIM = 64
GQA_GROUP = 4
WINDOW = 128
BLOCK = 128
POOL_WINDOWS = (2, 4, 8, 16)
RMS_EPS = 1e-6
NEG_INF = -1e30
LOG2E = 1.4426950408889634

LANES = 128
BF16_SUBLANES = 16
MXU_DIM = 256
HEADS_PER_LANE_GROUP = LANES // HEAD_DIM
BAND = 3 * BLOCK
POOL_HALO = 16

TOKEN_TILE = 512
FF_CHUNK = 256
VMEM_LIMIT = 56 * 1024 * 1024


def _params(n_axes, vmem=VMEM_LIMIT):
    return pltpu.CompilerParams(dimension_semantics=("arbitrary",) * n_axes, vmem_limit_bytes=vmem)


def _resident(shape):
    nd = len(shape)
    return pl.BlockSpec(shape, lambda *_: (0,) * nd, pipeline_mode=pl.Buffered(1))


def _rows(tm, d):
    return pl.BlockSpec((tm, d), lambda i: (i, 0))


def _rms_norm(x, gain):
    return x * lax.rsqrt(jnp.mean(x * x, axis=-1, keepdims=True) + RMS_EPS) * gain


def _dot(a, b):
    return jnp.dot(a, b, preferred_element_type=F32)


def _dot_nt(a, b):
    return lax.dot_general(a, b, (((1,), (1,)), ((), ())), preferred_element_type=F32)


def _head_mean_square(t, seg_ref):
    sq = t * t
    hi = sq.astype(BF16)
    lo = (sq - hi.astype(F32)).astype(BF16)
    seg = seg_ref[...]
    cols = []
    for c in range(t.shape[1] // MXU_DIM):
        sl = slice(c * MXU_DIM, (c + 1) * MXU_DIM)
        cols.append(_dot(hi[:, sl], seg) + _dot(lo[:, sl], seg))
    return cols[0] if len(cols) == 1 else jnp.concatenate(cols, axis=1)


def _project_qkv(h, gain_ref, w_ref, qgain_ref, kgain_ref, seg_ref, q_ref, k_ref, vt_ref):
    dq, dkv = q_ref.shape[1], k_ref.shape[1]
    hn = _rms_norm(h, gain_ref[...]).astype(BF16)
    qkv = _dot(hn, w_ref[...])
    q = qkv[:, :dq]
    k = qkv[:, dq:dq + dkv]
    v = qkv[:, dq + dkv:]
    q = q * lax.rsqrt(_head_mean_square(q, seg_ref) + RMS_EPS) * qgain_ref[...] * (HEAD_DIM ** -0.5 * LOG2E)
    k = k * lax.rsqrt(_head_mean_square(k, seg_ref) + RMS_EPS) * kgain_ref[...]
    q_ref[...] = q.astype(q_ref.dtype)
    k_ref[...] = k.astype(k_ref.dtype)
    for t in range(vt_ref.shape[0]):
        vt_ref[t] = v[t * BLOCK:(t + 1) * BLOCK, :].T.astype(vt_ref.dtype)


def _stage_kernel(*refs, d_ff, chunk, pre_proj, post_qkv, n_cast):
    refs = list(refs)

    def take(n):
        taken, refs[:] = refs[:n], refs[n:]
        return taken

    (x_ref,) = take(1)
    att_ref, wo_ref = take(2) if pre_proj else (None, None)
    gain_ref, wgu_ref, wd_ref = take(3)
    qkv_in = take(5) if post_qkv else []
    cast_in = take(n_cast)
    (out_ref,) = take(1)
    qkv_out = take(3) if post_qkv else []
    cast_out = take(n_cast)
    (act_ref,) = take(1)

    x = x_ref[...]
    if pre_proj:
        x = x + _dot(att_ref[...], wo_ref[...])
    xn = _rms_norm(x, gain_ref[...]).astype(BF16)
    for c in range(d_ff // chunk):
        gate = _dot(xn, wgu_ref[:, c * chunk:(c + 1) * chunk])
        up = _dot(xn, wgu_ref[:, d_ff + c * chunk:d_ff + (c + 1) * chunk])
        act_ref[:, c * chunk:(c + 1) * chunk] = (jax.nn.silu(gate) * up).astype(BF16)
    h = x + 0.5 * _dot(act_ref[...], wd_ref[...])
    out_ref[...] = h
    if post_qkv:
        _project_qkv(h, *qkv_in, *qkv_out)
    for src, dst in zip(cast_in, cast_out):
        dst[...] = src[...].astype(dst.dtype)


def _stage(h, gain, wgu, wd, att=None, wo=None, qkv=None, cast=()):
    t, d = h.shape
    d_ff = wd.shape[0]
    tm = min(TOKEN_TILE, t)
    steps = t // tm
    operands, in_specs = [h], [_rows(tm, d)]
    if att is not None:
        operands += [att, wo]
        in_specs += [_rows(tm, att.shape[1]), _resident(wo.shape)]
    operands += [gain, wgu, wd]
    in_specs += [_resident(gain.shape), _resident(wgu.shape), _resident(wd.shape)]
    out_shape, out_specs = [jax.ShapeDtypeStruct((t, d), h.dtype)], [_rows(tm, d)]
    if qkv is not None:
        operands += list(qkv)
        in_specs += [_resident(a.shape) for a in qkv]
        dq, dkv = qkv[2].shape[1], qkv[3].shape[1]
        out_shape += [jax.ShapeDtypeStruct((t, dq), BF16), jax.ShapeDtypeStruct((t, dkv), BF16),
                      jax.ShapeDtypeStruct((t // BLOCK, dkv, BLOCK), BF16)]
        out_specs += [_rows(tm, dq), _rows(tm, dkv), pl.BlockSpec((tm // BLOCK, dkv, BLOCK), lambda i: (i, 0, 0))]
    for w in cast:
        rows = w.shape[0] // steps
        operands.append(w)
        in_specs.append(_rows(rows, w.shape[1]))
        out_shape.append(jax.ShapeDtypeStruct(w.shape, BF16))
        out_specs.append(_rows(rows, w.shape[1]))
    return pl.pallas_call(
        functools.partial(_stage_kernel, d_ff=d_ff, chunk=FF_CHUNK, pre_proj=att is not None,
                          post_qkv=qkv is not None, n_cast=len(cast)),
        out_shape=out_shape,
        grid=(steps,),
        in_specs=in_specs,
        out_specs=out_specs,
        scratch_shapes=[pltpu.VMEM((tm, d_ff), BF16)],
        compiler_params=_params(1),
        name="stage",
    )(*operands)


def _slope2(head, n_heads):
    return 2.0 ** (-8.0 * (head + 1) / n_heads) * LOG2E


def _head_index(j, e, g):
    return (HEADS_PER_LANE_GROUP * j + e) * GQA_GROUP + g


def _stack_queries(q, j):
    rows = q.shape[0]
    low = lax.broadcasted_iota(jnp.int32, (rows, LANES), 1) < HEAD_DIM
    zero = jnp.zeros((rows, LANES), q.dtype)
    groups = [q[:, (j * GQA_GROUP + g) * LANES:(j * GQA_GROUP + g + 1) * LANES] for g in range(GQA_GROUP)]
    return jnp.concatenate([jnp.where(low, qg, zero) for qg in groups]
                           + [jnp.where(low, zero, qg) for qg in groups], axis=0)


def _attn_real_kernel(sink_ref, q_ref, k_ref, vt_ref, km_ref, vmt_ref, o_ref,
                      bias_l_ref, bias_c_ref, bias_r_ref, bias_m_ref, vall_ref, p_ref):
    first = (pl.program_id(0) == 0) & (pl.program_id(1) == 0)
    n = pl.program_id(1)
    last = pl.num_programs(1) - 1
    n_heads = q_ref.shape[1] // HEAD_DIM

    @pl.when(first)
    def _init_tables():
        key = lax.broadcasted_iota(jnp.int32, (BLOCK, BLOCK), 0)
        qry = lax.broadcasted_iota(jnp.int32, (BLOCK, BLOCK), 1)
        masked = jnp.full((BLOCK, BLOCK), NEG_INF, F32)
        meta = lax.broadcasted_iota(jnp.int32, (N_META, BLOCK), 0)
        mqry = lax.broadcasted_iota(jnp.int32, (N_META, BLOCK), 1)
        for h in range(n_heads):
            slope = _slope2(h, n_heads)
            for ref, delta in ((bias_l_ref, key - BLOCK - qry), (bias_c_ref, key - qry), (bias_r_ref, key + BLOCK - qry)):
                dist = jnp.abs(delta)
                table = jnp.where(dist <= WINDOW, -slope * dist.astype(F32), NEG_INF)
                if ref is bias_c_ref:
                    ref[h] = table
                else:
                    ref[0, h] = table
                    ref[1, h] = masked
            bias_m_ref[h] = -slope * (N_META + mqry - meta).astype(F32)
        vall_ref[...] = jnp.zeros_like(vall_ref)
        p_ref[...] = jnp.zeros_like(p_ref)

    left = jnp.maximum(n - 1, 0)
    right = jnp.minimum(n + 1, last)
    no_left = (n == 0).astype(jnp.int32)
    no_right = (n == last).astype(jnp.int32)
    blocks = (left, n, right)
    keys = jnp.concatenate([k_ref[pl.ds(pl.multiple_of(blk * BLOCK, BLOCK), BLOCK), :] for blk in blocks]
                           + [km_ref[...]], axis=0)
    for i, blk in enumerate(blocks):
        vall_ref[:, i * BLOCK:(i + 1) * BLOCK] = vt_ref[blk]
    vall_ref[:, BAND:BAND + N_META] = vmt_ref[...]
    q = q_ref[...]
    block_shift = (n * BLOCK).astype(F32)
    upper = lax.broadcasted_iota(jnp.int32, (LANES, BLOCK), 0) < HEAD_DIM

    low = lax.broadcasted_iota(jnp.int32, (BLOCK, LANES), 1) < HEAD_DIM
    zero = jnp.zeros((BLOCK, LANES), q.dtype)
    heads_per_dot = MXU_DIM // BLOCK
    for j in range(keys.shape[1] // LANES):
        kg = keys[:, j * LANES:(j + 1) * LANES]
        inv = []
        for e in range(HEADS_PER_LANE_GROUP):
            for g0 in range(0, GQA_GROUP, heads_per_dot):
                groups = [q[:, (j * GQA_GROUP + g) * LANES:(j * GQA_GROUP + g + 1) * LANES]
                          for g in range(g0, g0 + heads_per_dot)]
                halves = [jnp.where(low, qg, zero) if e == 0 else jnp.where(low, zero, qg) for qg in groups]
                st = _dot_nt(kg, jnp.concatenate(halves, axis=0))
                for gi in range(heads_per_dot):
                    head = _head_index(j, e, g0 + gi)
                    src = slice(gi * BLOCK, (gi + 1) * BLOCK)
                    cols = slice((e * GQA_GROUP + g0 + gi) * BLOCK, (e * GQA_GROUP + g0 + gi + 1) * BLOCK)
                    s_l = st[0:BLOCK, src] + bias_l_ref[no_left, head]
                    s_c = st[BLOCK:2 * BLOCK, src] + bias_c_ref[head]
                    s_r = st[2 * BLOCK:BAND, src] + bias_r_ref[no_right, head]
                    s_m = st[BAND:BAND + N_META, src] + (bias_m_ref[head] - _slope2(head, n_heads) * block_shift)
                    sink = sink_ref[head] * LOG2E
                    m = jnp.max(jnp.maximum(jnp.maximum(s_l, s_c), s_r), axis=0, keepdims=True)
                    m = jnp.maximum(jnp.maximum(m, jnp.max(s_m, axis=0, keepdims=True)), sink)
                    e_l, e_c, e_r, e_m = (jnp.exp2(s - m) for s in (s_l, s_c, s_r, s_m))
                    denom = (jnp.sum(e_l + e_c + e_r, axis=0, keepdims=True) + jnp.sum(e_m, axis=0, keepdims=True)
                             + jnp.exp2(sink - m))
                    inv.append(1.0 / denom)
                    p_ref[j, 0:BLOCK, cols] = e_l.astype(p_ref.dtype)
                    p_ref[j, BLOCK:2 * BLOCK, cols] = e_c.astype(p_ref.dtype)
                    p_ref[j, 2 * BLOCK:BAND, cols] = e_r.astype(p_ref.dtype)
                    p_ref[j, BAND:BAND + N_META, cols] = e_m.astype(p_ref.dtype)
        out_t = _dot(vall_ref[j * LANES:(j + 1) * LANES, :], p_ref[j])
        for g in range(GQA_GROUP):
            c0, c1 = g * BLOCK, (GQA_GROUP + g) * BLOCK
            og_t = jnp.where(upper, out_t[:, c0:c0 + BLOCK] * inv[g], out_t[:, c1:c1 + BLOCK] * inv[GQA_GROUP + g])
            lanes = slice((j * GQA_GROUP + g) * LANES, (j * GQA_GROUP + g + 1) * LANES)
            o_ref[:, lanes] = og_t.T.astype(o_ref.dtype)


def _attn_meta_kernel(sink_ref, q_ref, k_ref, vt_ref, km_ref, vm_ref, o_ref):
    rows = q_ref.shape[0]
    n_heads = q_ref.shape[1] // HEAD_DIM
    keys = jnp.concatenate([k_ref[...], km_ref[...]], axis=0)
    vals = jnp.concatenate([vt_ref[0].astype(F32).T.astype(BF16), vm_ref[...]], axis=0)
    nk = BLOCK + N_META
    qpos = lax.broadcasted_iota(jnp.int32, (rows, nk), 0)
    col = lax.broadcasted_iota(jnp.int32, (rows, nk), 1)
    idist = jnp.abs(qpos - jnp.where(col < BLOCK, N_META + col, col - BLOCK))
    valid = idist <= WINDOW
    dist = idist.astype(F32)
    low = lax.broadcasted_iota(jnp.int32, (rows, LANES), 1) < HEAD_DIM
    q = q_ref[...]
    for j in range(keys.shape[1] // LANES):
        scores = _dot_nt(_stack_queries(q, j), keys[:, j * LANES:(j + 1) * LANES])
        probs = []
        for e in range(HEADS_PER_LANE_GROUP):
            for g in range(GQA_GROUP):
                head = _head_index(j, e, g)
                r0 = (e * GQA_GROUP + g) * rows
                s = jnp.where(valid, scores[r0:r0 + rows] - _slope2(head, n_heads) * dist, NEG_INF)
                sink = sink_ref[head] * LOG2E
                m = jnp.maximum(jnp.max(s, axis=-1, keepdims=True), sink)
                ex = jnp.exp2(s - m)
                denom = jnp.sum(ex, axis=-1, keepdims=True) + jnp.exp2(sink - m)
                probs.append((ex * (1.0 / denom)).astype(BF16))
        out = _dot(jnp.concatenate(probs, axis=0), vals[:, j * LANES:(j + 1) * LANES])
        for g in range(GQA_GROUP):
            og = jnp.where(low, out[g * rows:(g + 1) * rows], out[(GQA_GROUP + g) * rows:(GQA_GROUP + g + 1) * rows])
            lanes = slice((j * GQA_GROUP + g) * LANES, (j * GQA_GROUP + g + 1) * LANES)
            o_ref[:, lanes] = og.astype(o_ref.dtype)


def _attention(sink, q, k, vt, qm, km, vm, vmt):
    b, s, dq = q.shape
    dkv = k.shape[2]
    nb = s // BLOCK
    n_heads = dq // HEAD_DIM
    n_pairs = dkv // LANES
    padded_keys = 2 * MXU_DIM
    smem = pl.BlockSpec(memory_space=pltpu.SMEM)
    o_real = pl.pallas_call(
        _attn_real_kernel,
        out_shape=jax.ShapeDtypeStruct((b, s, dq), BF16),
        grid=(b, nb),
        in_specs=[smem,
                  pl.BlockSpec((None, BLOCK, dq), lambda i, n: (i, n, 0)),
                  pl.BlockSpec((None, s, dkv), lambda i, n: (i, 0, 0)),
                  pl.BlockSpec((nb, dkv, BLOCK), lambda i, n: (i, 0, 0)),
                  pl.BlockSpec((None, N_META, dkv), lambda i, n: (i, 0, 0)),
                  pl.BlockSpec((None, dkv, N_META), lambda i, n: (i, 0, 0))],
        out_specs=pl.BlockSpec((None, BLOCK, dq), lambda i, n: (i, n, 0)),
        scratch_shapes=[pltpu.VMEM((2, n_heads, BLOCK, BLOCK), F32), pltpu.VMEM((n_heads, BLOCK, BLOCK), F32),
                        pltpu.VMEM((2, n_heads, BLOCK, BLOCK), F32), pltpu.VMEM((n_heads, N_META, BLOCK), F32),
                        pltpu.VMEM((dkv, padded_keys), BF16),
                        pltpu.VMEM((n_pairs, padded_keys, 2 * GQA_GROUP * BLOCK), BF16)],
        compiler_params=_params(2),
        name="attn_real",
    )(sink, q, k, vt, km, vmt)
    o_meta = pl.pallas_call(
        _attn_meta_kernel,
        out_shape=jax.ShapeDtypeStruct((b, N_META, dq), BF16),
        grid=(b,),
        in_specs=[smem, pl.BlockSpec((None, N_META, dq), lambda i: (i, 0, 0)),
                  pl.BlockSpec((None, BLOCK, dkv), lambda i: (i, 0, 0)),
                  pl.BlockSpec((1, dkv, BLOCK), lambda i: (i * nb, 0, 0)),
                  pl.BlockSpec((None, N_META, dkv), lambda i: (i, 0, 0)),
                  pl.BlockSpec((None, N_META, dkv), lambda i: (i, 0, 0))],
        out_specs=pl.BlockSpec((None, N_META, dq), lambda i: (i, 0, 0)),
        compiler_params=_params(1),
        name="attn_meta",
    )(sink, qm, k, vt, km, vm)
    return o_real, o_meta


def _pool_kernel(h_ref, prev_ref, next_ref, meta_ref, gain_ref, win_ref, wgrp_ref, scale_ref, wout_ref, o_ref,
                 hn_ref, u_ref, *, tm, total_len):
    i = pl.program_id(1)
    last = pl.num_programs(1) - 1
    gain = gain_ref[...]
    before = jnp.where(i == 0, meta_ref[...], prev_ref[...])
    hn_ref[0:POOL_HALO, :] = _rms_norm(before, gain).astype(BF16)
    hn_ref[POOL_HALO:POOL_HALO + tm, :] = _rms_norm(h_ref[...], gain).astype(BF16)
    after = _rms_norm(next_ref[...], gain)
    hn_ref[POOL_HALO + tm:, :] = jnp.where(i == last, 0.0, after).astype(BF16)
    u_ref[...] = _dot(hn_ref[...], win_ref[...])

    n_groups = len(POOL_WINDOWS)
    gdim = u_ref.shape[1] // n_groups
    pos = N_META + i * tm + lax.broadcasted_iota(jnp.int32, (tm, 1), 0)
    mixed = []
    for g, window in enumerate(POOL_WINDOWS):
        half = window // 2
        cols = slice(g * gdim, (g + 1) * gdim)
        total = u_ref[POOL_HALO - half:POOL_HALO - half + tm, cols]
        for off in range(1 - half, half):
            total = total + u_ref[POOL_HALO + off:POOL_HALO + off + tm, cols]
        count = (half + jnp.minimum(half, total_len - pos)).astype(F32)
        pooled = total / count - u_ref[POOL_HALO:POOL_HALO + tm, cols]
        mixed.append(_dot(pooled.astype(BF16), wgrp_ref[g]))
    y = (jnp.concatenate(mixed, axis=1) * scale_ref[...]).astype(BF16)
    o_ref[...] = h_ref[...] + _dot(y, wout_ref[...])


def _pool_mixer(h, h_meta, gain, w_in, w_grp, scale, w_out):
    b, s, d = h.shape
    tm = TOKEN_TILE
    per_tile = tm // POOL_HALO
    n_halo_blocks = s // POOL_HALO
    return pl.pallas_call(
        functools.partial(_pool_kernel, tm=tm, total_len=N_META + s),
        out_shape=jax.ShapeDtypeStruct((b, s, d), h.dtype),
        grid=(b, s // tm),
        in_specs=[pl.BlockSpec((None, tm, d), lambda bi, i: (bi, i, 0)),
                  pl.BlockSpec((None, POOL_HALO, d), lambda bi, i: (bi, jnp.maximum(i * per_tile - 1, 0), 0)),
                  pl.BlockSpec((None, POOL_HALO, d),
                               lambda bi, i: (bi, jnp.minimum((i + 1) * per_tile, n_halo_blocks - 1), 0)),
                  pl.BlockSpec((None, N_META, d), lambda bi, i: (bi, 0, 0)),
                  _resident((1, d)), _resident(w_in.shape), _resident(w_grp.shape), _resident((1, d)),
                  _resident(w_out.shape)],
        out_specs=pl.BlockSpec((None, tm, d), lambda bi, i: (bi, i, 0)),
        scratch_shapes=[pltpu.VMEM((tm + 2 * POOL_HALO, d), BF16), pltpu.VMEM((tm + 2 * POOL_HALO, d), F32)],
        compiler_params=_params(2),
        name="pool_mixer",
    )(h, h, h, h_meta, gain, w_in, w_grp, scale, w_out)


def _head_permutation(n_heads):
    n_kv = n_heads // GQA_GROUP
    cols = []
    for j in range(n_kv // HEADS_PER_LANE_GROUP):
        for g in range(GQA_GROUP):
            for e in range(HEADS_PER_LANE_GROUP):
                head = (HEADS_PER_LANE_GROUP * j + e) * GQA_GROUP + g
                cols.extend(range(head * HEAD_DIM, (head + 1) * HEAD_DIM))
    return np.asarray(cols, dtype=np.int32)


def _segment_mean_matrix():
    seg = np.kron(np.eye(MXU_DIM // HEAD_DIM), np.ones((HEAD_DIM, HEAD_DIM))) / HEAD_DIM
    return jnp.asarray(seg, dtype=BF16)


def kernel(x, meta_tokens, ffn_norm, w_gate_up, w_down, mixer_norm, w_qkv, q_norm, k_norm, sink_logit, w_o,
           w_pool_in, w_pool_group, pool_scale, w_pool_out):
    b, s, d = x.shape
    depth = ffn_norm.shape[0]
    n_heads = sink_logit.shape[1]
    dq = n_heads * HEAD_DIM
    dkv = dq // GQA_GROUP
    assert depth == 2 and s % TOKEN_TILE == 0 and d == dq
    perm = _head_permutation(n_heads)
    seg = _segment_mean_matrix()

    hr = x.reshape(b * s, d)
    hm = jnp.broadcast_to(meta_tokens[None].astype(x.dtype), (b, N_META, d)).reshape(b * N_META, d)
    steps = (b * s) // TOKEN_TILE
    cast_rows = steps * BF16_SUBLANES

    def cast_view(w):
        return w.reshape(cast_rows, w.size // cast_rows)

    def ffn_weights(layer, which):
        return cast_view(w_gate_up[layer, which]), cast_view(w_down[layer, which])

    def restore(wgu_wd, layer, which):
        return wgu_wd[0].reshape(w_gate_up[layer, which].shape), wgu_wd[1].reshape(w_down[layer, which].shape)

    def gain(layer, which):
        return ffn_norm[layer, which][None]

    w00 = w_gate_up[0, 0].astype(BF16), w_down[0, 0].astype(BF16)
    w_qkv_p = jnp.concatenate([w_qkv[0][:, :dq][:, perm], w_qkv[0][:, dq:]], axis=1).astype(BF16)
    w_o_p = w_o[0][perm, :].astype(BF16)
    qkv_args = (mixer_norm[0][None], w_qkv_p, jnp.tile(q_norm[0], n_heads)[None],
                jnp.tile(k_norm[0], dkv // HEAD_DIM)[None], seg)

    hr, q, k, vt, *w01 = _stage(hr, gain(0, 0), *w00, qkv=qkv_args, cast=ffn_weights(0, 1))
    hm, qm, km, vmt = _stage(hm, gain(0, 0), *w00, qkv=qkv_args)
    w01 = restore(w01, 0, 1)
    vm = vmt.transpose(0, 2, 1).reshape(b, N_META, dkv)
    o_real, o_meta = _attention(sink_logit[0], q.reshape(b, s, dq), k.reshape(b, s, dkv), vt,
                                qm.reshape(b, N_META, dq), km.reshape(b, N_META, dkv), vm, vm.transpose(0, 2, 1))
    hr, *w10 = _stage(hr, gain(0, 1), *w01, att=o_real.reshape(b * s, dq), wo=w_o_p, cast=ffn_weights(1, 0))
    (hm,) = _stage(hm, gain(0, 1), *w01, att=o_meta.reshape(b * N_META, dq), wo=w_o_p)
    w10 = restore(w10, 1, 0)

    pool_f32 = (w_pool_in[0], w_pool_group[0], w_pool_out[0])
    hr, *casts = _stage(hr, gain(1, 0), *w10, cast=ffn_weights(1, 1) + tuple(cast_view(w) for w in pool_f32))
    (hm,) = _stage(hm, gain(1, 0), *w10)
    w11 = restore(casts[:2], 1, 1)
    w_in, w_grp, w_out = (c.reshape(w.shape) for c, w in zip(casts[2:], pool_f32))
    hr = _pool_mixer(hr.reshape(b, s, d), hm.reshape(b, N_META, d), mixer_norm[1][None], w_in, w_grp,
                     pool_scale[0][None], w_out)
    (hr,) = _stage(hr.reshape(b * s, d), gain(1, 1), *w11)
    return hr.reshape(b, s, d)
```

```python
import functools

import numpy as np
import jax
import jax.numpy as jnp
from jax import lax
from jax.experimental import pallas as pl
from jax.experimental.pallas import tpu as pltpu

F32 = jnp.float32
BF16 = jnp.bfloat16

N_META = 16
HEAD_DIM = 64
GQA_GROUP = 4
WINDOW = 128
BLOCK = 128
POOL_WINDOWS = (2, 4, 8, 16)
RMS_EPS = 1e-6
NEG_INF = -1e30
LOG2E = 1.4426950408889634

LANES = 128
F32_SUBLANES = 8
BF16_SUBLANES = 16
MXU_DIM = 256
HEADS_PER_LANE_GROUP = LANES // HEAD_DIM
BAND = 3 * BLOCK
POOL_HALO = 16

TOKEN_TILE = 512
FF_CHUNK = 256
VMEM_LIMIT = 56 * 1024 * 1024


def _params(n_axes, vmem=VMEM_LIMIT):
    return pltpu.CompilerParams(dimension_semantics=("arbitrary",) * n_axes, vmem_limit_bytes=vmem)


def _resident(shape):
    nd = len(shape)
    return pl.BlockSpec(shape, lambda *_: (0,) * nd, pipeline_mode=pl.Buffered(1))


def _rows(tm, d):
    return pl.BlockSpec((tm, d), lambda i: (i, 0))


def _rms_norm(x, gain):
    return x * lax.rsqrt(jnp.mean(x * x, axis=-1, keepdims=True) + RMS_EPS) * gain


def _dot(a, b):
    return jnp.dot(a, b, preferred_element_type=F32)


def _dot_nt(a, b):
    return lax.dot_general(a, b, (((1,), (1,)), ((), ())), preferred_element_type=F32)


def _head_mean_square(t, seg_ref):
    sq = t * t
    hi = sq.astype(BF16)
    lo = (sq - hi.astype(F32)).astype(BF16)
    seg = seg_ref[...]
    cols = []
    for c in range(t.shape[1] // MXU_DIM):
        sl = slice(c * MXU_DIM, (c + 1) * MXU_DIM)
        cols.append(_dot(hi[:, sl], seg) + _dot(lo[:, sl], seg))
    return cols[0] if len(cols) == 1 else jnp.concatenate(cols, axis=1)


def _project_qkv(h, gain_ref, w_ref, qgain_ref, kgain_ref, seg_ref, q_ref, k_ref, vt_ref):
    dq, dkv = q_ref.shape[1], k_ref.shape[1]
    hn = _rms_norm(h, gain_ref[...]).astype(BF16)
    qkv = _dot(hn, w_ref[...])
    q = qkv[:, :dq]
    k = qkv[:, dq:dq + dkv]
    v = qkv[:, dq + dkv:]
    q = q * lax.rsqrt(_head_mean_square(q, seg_ref) + RMS_EPS) * qgain_ref[...] * (HEAD_DIM ** -0.5 * LOG2E)
    k = k * lax.rsqrt(_head_mean_square(k, seg_ref) + RMS_EPS) * kgain_ref[...]
    q_ref[...] = q.astype(q_ref.dtype)
    k_ref[...] = k.astype(k_ref.dtype)
    for t in range(vt_ref.shape[0]):
        vt_ref[t] = v[t * BLOCK:(t + 1) * BLOCK, :].T.astype(vt_ref.dtype)


def _stage_kernel(*refs, d_ff, chunk, pre_proj, post_qkv, n_cast):
    refs = list(refs)

    def take(n):
        taken, refs[:] = refs[:n], refs[n:]
        return taken

    (x_ref,) = take(1)
    att_ref, wo_ref = take(2) if pre_proj else (None, None)
    gain_ref, wgu_ref, wd_ref = take(3)
    qkv_in = take(5) if post_qkv else []
    cast_in = take(n_cast)
    (out_ref,) = take(1)
    qkv_out = take(3) if post_qkv else []
    cast_out = take(n_cast)
    (act_ref,) = take(1)

    x = x_ref[...]
    if pre_proj:
        x = x + _dot(att_ref[...], wo_ref[...])
    xn = _rms_norm(x, gain_ref[...]).astype(BF16)
    for c in range(d_ff // chunk):
        gate = _dot(xn, wgu_ref[:, c * chunk:(c + 1) * chunk])
        up = _dot(xn, wgu_ref[:, d_ff + c * chunk:d_ff + (c + 1) * chunk])
        act_ref[:, c * chunk:(c + 1) * chunk] = (jax.nn.silu(gate) * up).astype(BF16)
    h = x + 0.5 * _dot(act_ref[...], wd_ref[...])
    out_ref[...] = h
    if post_qkv:
        _project_qkv(h, *qkv_in, *qkv_out)
    for src, dst in zip(cast_in, cast_out):
        dst[...] = src[...].astype(dst.dtype)


def _cast_specs(rows, cols, prefix, steps):
    block = next(r for r in range(BF16_SUBLANES, rows + 1, BF16_SUBLANES) if rows % r == 0 and r * steps >= rows)
    last = rows // block - 1
    in_spec = pl.BlockSpec((None,) * len(prefix) + (block, cols), lambda i: (*prefix, jnp.minimum(i, last), 0))
    return in_spec, pl.BlockSpec((block, cols), lambda i: (jnp.minimum(i, last), 0))


def _stage(h, gain, wgu, wd, att=None, wo=None, qkv=None, cast=()):
    t, d = h.shape
    d_ff = wd.shape[0]
    tm = min(TOKEN_TILE, t)
    steps = t // tm
    operands, in_specs = [h], [_rows(tm, d)]
    if att is not None:
        operands += [att, wo]
        in_specs += [_rows(tm, att.shape[1]), _resident(wo.shape)]
    operands += [gain, wgu, wd]
    in_specs += [_resident(gain.shape), _resident(wgu.shape), _resident(wd.shape)]
    out_shape, out_specs = [jax.ShapeDtypeStruct((t, d), h.dtype)], [_rows(tm, d)]
    if qkv is not None:
        operands += list(qkv)
        in_specs += [_resident(a.shape) for a in qkv]
        dq, dkv = qkv[2].shape[1], qkv[3].shape[1]
        out_shape += [jax.ShapeDtypeStruct((t, dq), BF16), jax.ShapeDtypeStruct((t, dkv), BF16),
                      jax.ShapeDtypeStruct((t // BLOCK, dkv, BLOCK), BF16)]
        out_specs += [_rows(tm, dq), _rows(tm, dkv), pl.BlockSpec((tm // BLOCK, dkv, BLOCK), lambda i: (i, 0, 0))]
    for w, prefix in cast:
        in_spec, out_spec = _cast_specs(w.shape[-2], w.shape[-1], prefix, steps)
        operands.append(w)
        in_specs.append(in_spec)
        out_shape.append(jax.ShapeDtypeStruct(w.shape[-2:], BF16))
        out_specs.append(out_spec)
    return pl.pallas_call(
        functools.partial(_stage_kernel, d_ff=d_ff, chunk=FF_CHUNK, pre_proj=att is not None,
                          post_qkv=qkv is not None, n_cast=len(cast)),
        out_shape=out_shape,
        grid=(steps,),
        in_specs=in_specs,
        out_specs=out_specs,
        scratch_shapes=[pltpu.VMEM((tm, d_ff), BF16)],
        compiler_params=_params(1),
        name="stage",
    )(*operands)


def _slope2(head, n_heads):
    return 2.0 ** (-8.0 * (head + 1) / n_heads) * LOG2E


def _head_index(j, e, g):
    return (HEADS_PER_LANE_GROUP * j + e) * GQA_GROUP + g


def _stack_queries(q, j):
    rows = q.shape[0]
    low = lax.broadcasted_iota(jnp.int32, (rows, LANES), 1) < HEAD_DIM
    zero = jnp.zeros((rows, LANES), q.dtype)
    groups = [q[:, (j * GQA_GROUP + g) * LANES:(j * GQA_GROUP + g + 1) * LANES] for g in range(GQA_GROUP)]
    return jnp.concatenate([jnp.where(low, qg, zero) for qg in groups]
                           + [jnp.where(low, zero, qg) for qg in groups], axis=0)


def _attn_real_kernel(sink_ref, q_ref, k_ref, vt_ref, km_ref, vmt_ref, o_ref,
                      bias_l_ref, bias_c_ref, bias_r_ref, bias_m_ref, vall_ref, p_ref):
    first = (pl.program_id(0) == 0) & (pl.program_id(1) == 0)
    n = pl.program_id(1)
    last = pl.num_programs(1) - 1
    n_heads = q_ref.shape[1] // HEAD_DIM

    @pl.when(first)
    def _init_tables():
        key = lax.broadcasted_iota(jnp.int32, (BLOCK, BLOCK), 0)
        qry = lax.broadcasted_iota(jnp.int32, (BLOCK, BLOCK), 1)
        masked = jnp.full((BLOCK, BLOCK), NEG_INF, F32)
        meta = lax.broadcasted_iota(jnp.int32, (N_META, BLOCK), 0)
        mqry = lax.broadcasted_iota(jnp.int32, (N_META, BLOCK), 1)
        for h in range(n_heads):
            slope = _slope2(h, n_heads)
            for ref, delta in ((bias_l_ref, key - BLOCK - qry), (bias_c_ref, key - qry), (bias_r_ref, key + BLOCK - qry)):
                dist = jnp.abs(delta)
                table = jnp.where(dist <= WINDOW, -slope * dist.astype(F32), NEG_INF)
                if ref is bias_c_ref:
                    ref[h] = table
                else:
                    ref[0, h] = table
                    ref[1, h] = masked
            bias_m_ref[h] = -slope * (N_META + mqry - meta).astype(F32)
        vall_ref[...] = jnp.zeros_like(vall_ref)
        p_ref[...] = jnp.zeros_like(p_ref)

    left = jnp.maximum(n - 1, 0)
    right = jnp.minimum(n + 1, last)
    no_left = (n == 0).astype(jnp.int32)
    no_right = (n == last).astype(jnp.int32)
    blocks = (left, n, right)
    keys = jnp.concatenate([k_ref[pl.ds(pl.multiple_of(blk * BLOCK, BLOCK), BLOCK), :] for blk in blocks]
                           + [km_ref[...]], axis=0)
    for i, blk in enumerate(blocks):
        vall_ref[:, i * BLOCK:(i + 1) * BLOCK] = vt_ref[blk]
    vall_ref[:, BAND:BAND + N_META] = vmt_ref[...]
    q = q_ref[...]
    block_shift = (n * BLOCK).astype(F32)
    upper = lax.broadcasted_iota(jnp.int32, (LANES, BLOCK), 0) < HEAD_DIM

    low = lax.broadcasted_iota(jnp.int32, (BLOCK, LANES), 1) < HEAD_DIM
    zero = jnp.zeros((BLOCK, LANES), q.dtype)
    heads_per_dot = MXU_DIM // BLOCK
    for j in range(keys.shape[1] // LANES):
        kg = keys[:, j * LANES:(j + 1) * LANES]
        inv = []
        for e in range(HEADS_PER_LANE_GROUP):
            for g0 in range(0, GQA_GROUP, heads_per_dot):
                groups = [q[:, (j * GQA_GROUP + g) * LANES:(j * GQA_GROUP + g + 1) * LANES]
                          for g in range(g0, g0 + heads_per_dot)]
                halves = [jnp.where(low, qg, zero) if e == 0 else jnp.where(low, zero, qg) for qg in groups]
                st = _dot_nt(kg, jnp.concatenate(halves, axis=0))
                for gi in range(heads_per_dot):
                    head = _head_index(j, e, g0 + gi)
                    src = slice(gi * BLOCK, (gi + 1) * BLOCK)
                    cols = slice((e * GQA_GROUP + g0 + gi) * BLOCK, (e * GQA_GROUP + g0 + gi + 1) * BLOCK)
                    s_l = st[0:BLOCK, src] + bias_l_ref[no_left, head]
                    s_c = st[BLOCK:2 * BLOCK, src] + bias_c_ref[head]
                    s_r = st[2 * BLOCK:BAND, src] + bias_r_ref[no_right, head]
                    s_m = st[BAND:BAND + N_META, src] + (bias_m_ref[head] - _slope2(head, n_heads) * block_shift)
                    sink = sink_ref[head] * LOG2E
                    m = jnp.max(jnp.maximum(jnp.maximum(s_l, s_c), s_r), axis=0, keepdims=True)
                    m = jnp.maximum(jnp.maximum(m, jnp.max(s_m, axis=0, keepdims=True)), sink)
                    e_l, e_c, e_r, e_m = (jnp.exp2(s - m) for s in (s_l, s_c, s_r, s_m))
                    denom = (jnp.sum(e_l + e_c + e_r, axis=0, keepdims=True) + jnp.sum(e_m, axis=0, keepdims=True)
                             + jnp.exp2(sink - m))
                    inv.append(1.0 / denom)
                    p_ref[j, 0:BLOCK, cols] = e_l.astype(p_ref.dtype)
                    p_ref[j, BLOCK:2 * BLOCK, cols] = e_c.astype(p_ref.dtype)
                    p_ref[j, 2 * BLOCK:BAND, cols] = e_r.astype(p_ref.dtype)
                    p_ref[j, BAND:BAND + N_META, cols] = e_m.astype(p_ref.dtype)
        out_t = _dot(vall_ref[j * LANES:(j + 1) * LANES, :], p_ref[j])
        for g in range(GQA_GROUP):
            c0, c1 = g * BLOCK, (GQA_GROUP + g) * BLOCK
            og_t = jnp.where(upper, out_t[:, c0:c0 + BLOCK] * inv[g], out_t[:, c1:c1 + BLOCK] * inv[GQA_GROUP + g])
            lanes = slice((j * GQA_GROUP + g) * LANES, (j * GQA_GROUP + g + 1) * LANES)
            o_ref[:, lanes] = og_t.T.astype(o_ref.dtype)


def _attn_meta_kernel(sink_ref, q_ref, k_ref, vt_ref, km_ref, vm_ref, o_ref):
    rows = q_ref.shape[0]
    n_heads = q_ref.shape[1] // HEAD_DIM
    keys = jnp.concatenate([k_ref[...], km_ref[...]], axis=0)
    vals = jnp.concatenate([vt_ref[0].astype(F32).T.astype(BF16), vm_ref[...]], axis=0)
    nk = BLOCK + N_META
    qpos = lax.broadcasted_iota(jnp.int32, (rows, nk), 0)
    col = lax.broadcasted_iota(jnp.int32, (rows, nk), 1)
    idist = jnp.abs(qpos - jnp.where(col < BLOCK, N_META + col, col - BLOCK))
    valid = idist <= WINDOW
    dist = idist.astype(F32)
    low = lax.broadcasted_iota(jnp.int32, (rows, LANES), 1) < HEAD_DIM
    q = q_ref[...]
    for j in range(keys.shape[1] // LANES):
        scores = _dot_nt(_stack_queries(q, j), keys[:, j * LANES:(j + 1) * LANES])
        probs = []
        for e in range(HEADS_PER_LANE_GROUP):
            for g in range(GQA_GROUP):
                head = _head_index(j, e, g)
                r0 = (e * GQA_GROUP + g) * rows
                s = jnp.where(valid, scores[r0:r0 + rows] - _slope2(head, n_heads) * dist, NEG_INF)
                sink = sink_ref[head] * LOG2E
                m = jnp.maximum(jnp.max(s, axis=-1, keepdims=True), sink)
                ex = jnp.exp2(s - m)
                denom = jnp.sum(ex, axis=-1, keepdims=True) + jnp.exp2(sink - m)
                probs.append((ex * (1.0 / denom)).astype(BF16))
        out = _dot(jnp.concatenate(probs, axis=0), vals[:, j * LANES:(j + 1) * LANES])
        for g in range(GQA_GROUP):
            og = jnp.where(low, out[g * rows:(g + 1) * rows], out[(GQA_GROUP + g) * rows:(GQA_GROUP + g + 1) * rows])
            lanes = slice((j * GQA_GROUP + g) * LANES, (j * GQA_GROUP + g + 1) * LANES)
            o_ref[:, lanes] = og.astype(o_ref.dtype)


def _attention(sink, q, k, vt, qm, km, vm, vmt):
    b, s, dq = q.shape
    dkv = k.shape[2]
    nb = s // BLOCK
    n_heads = dq // HEAD_DIM
    n_pairs = dkv // LANES
    padded_keys = 2 * MXU_DIM
    smem = pl.BlockSpec(memory_space=pltpu.SMEM)
    o_real = pl.pallas_call(
        _attn_real_kernel,
        out_shape=jax.ShapeDtypeStruct((b, s, dq), BF16),
        grid=(b, nb),
        in_specs=[smem,
                  pl.BlockSpec((None, BLOCK, dq), lambda i, n: (i, n, 0)),
                  pl.BlockSpec((None, s, dkv), lambda i, n: (i, 0, 0)),
                  pl.BlockSpec((nb, dkv, BLOCK), lambda i, n: (i, 0, 0)),
                  pl.BlockSpec((None, N_META, dkv), lambda i, n: (i, 0, 0)),
                  pl.BlockSpec((None, dkv, N_META), lambda i, n: (i, 0, 0))],
        out_specs=pl.BlockSpec((None, BLOCK, dq), lambda i, n: (i, n, 0)),
        scratch_shapes=[pltpu.VMEM((2, n_heads, BLOCK, BLOCK), F32), pltpu.VMEM((n_heads, BLOCK, BLOCK), F32),
                        pltpu.VMEM((2, n_heads, BLOCK, BLOCK), F32), pltpu.VMEM((n_heads, N_META, BLOCK), F32),
                        pltpu.VMEM((dkv, padded_keys), BF16),
                        pltpu.VMEM((n_pairs, padded_keys, 2 * GQA_GROUP * BLOCK), BF16)],
        compiler_params=_params(2),
        name="attn_real",
    )(sink, q, k, vt, km, vmt)
    o_meta = pl.pallas_call(
        _attn_meta_kernel,
        out_shape=jax.ShapeDtypeStruct((b, N_META, dq), BF16),
        grid=(b,),
        in_specs=[smem, pl.BlockSpec((None, N_META, dq), lambda i: (i, 0, 0)),
                  pl.BlockSpec((None, BLOCK, dkv), lambda i: (i, 0, 0)),
                  pl.BlockSpec((1, dkv, BLOCK), lambda i: (i * nb, 0, 0)),
                  pl.BlockSpec((None, N_META, dkv), lambda i: (i, 0, 0)),
                  pl.BlockSpec((None, N_META, dkv), lambda i: (i, 0, 0))],
        out_specs=pl.BlockSpec((None, N_META, dq), lambda i: (i, 0, 0)),
        compiler_params=_params(1),
        name="attn_meta",
    )(sink, qm, k, vt, km, vm)
    return o_real, o_meta


def _pool_kernel(h_ref, prev_ref, next_ref, meta_ref, gain_ref, win_ref, wgrp_ref, scale_ref, wout_ref, o_ref,
                 hn_ref, u_ref, *run_refs, tm, total_len):
    i = pl.program_id(1)
    last = pl.num_programs(1) - 1
    gain = gain_ref[...]
    before = jnp.where(i == 0, meta_ref[...], prev_ref[...])
    hn_ref[0:POOL_HALO, :] = _rms_norm(before, gain).astype(BF16)
    hn_ref[POOL_HALO:POOL_HALO + tm, :] = _rms_norm(h_ref[...], gain).astype(BF16)
    after = _rms_norm(next_ref[...], gain)
    hn_ref[POOL_HALO + tm:, :] = jnp.where(i == last, 0.0, after).astype(BF16)
    span = tm + 2 * POOL_HALO
    u_ref[0:span, :] = _dot(hn_ref[...], win_ref[...])
    u_ref[span:, :] = jnp.zeros((u_ref.shape[0] - span, u_ref.shape[1]), F32)

    gdim = u_ref.shape[1] // len(POOL_WINDOWS)
    runs = {1: (u_ref, 0)}
    src_ref, src_col, width = u_ref, 0, 1
    for k, run_ref in enumerate(run_refs, start=1):
        col = k * gdim
        rows = span - F32_SUBLANES * k
        lo = F32_SUBLANES
        run_ref[lo:lo + rows, :] = (src_ref[lo:lo + rows, col - src_col:]
                                    + src_ref[lo + width:lo + width + rows, col - src_col:])
        src_ref, src_col, width = run_ref, col, 2 * width
        runs[width] = (run_ref, col)

    pos = N_META + i * tm + lax.broadcasted_iota(jnp.int32, (tm, 1), 0)
    mixed = []
    for g, window in enumerate(POOL_WINDOWS):
        half = window // 2
        run_ref, col = runs[half]
        cols = slice(g * gdim - col, (g + 1) * gdim - col)
        total = run_ref[POOL_HALO - half:POOL_HALO - half + tm, cols] + run_ref[POOL_HALO:POOL_HALO + tm, cols]
        count = (half + jnp.minimum(half, total_len - pos)).astype(F32)
        pooled = total / count - u_ref[POOL_HALO:POOL_HALO + tm, g * gdim:(g + 1) * gdim]
        mixed.append(_dot(pooled.astype(BF16), wgrp_ref[g]))
    y = (jnp.concatenate(mixed, axis=1) * scale_ref[...]).astype(BF16)
    o_ref[...] = h_ref[...] + _dot(y, wout_ref[...])


def _pool_mixer(h, h_meta, gain, w_in, w_grp, scale, w_out):
    b, s, d = h.shape
    tm = TOKEN_TILE
    per_tile = tm // POOL_HALO
    n_halo_blocks = s // POOL_HALO
    span = tm + 2 * POOL_HALO
    n_groups = len(POOL_WINDOWS)
    gdim = d // n_groups
    assert POOL_WINDOWS == tuple(2 ** (g + 1) for g in range(n_groups)) and POOL_WINDOWS[-1] // 2 <= F32_SUBLANES
    return pl.pallas_call(
        functools.partial(_pool_kernel, tm=tm, total_len=N_META + s),
        out_shape=jax.ShapeDtypeStruct((b, s, d), h.dtype),
        grid=(b, s // tm),
        in_specs=[pl.BlockSpec((None, tm, d), lambda bi, i: (bi, i, 0)),
                  pl.BlockSpec((None, POOL_HALO, d), lambda bi, i: (bi, jnp.maximum(i * per_tile - 1, 0), 0)),
                  pl.BlockSpec((None, POOL_HALO, d),
                               lambda bi, i: (bi, jnp.minimum((i + 1) * per_tile, n_halo_blocks - 1), 0)),
                  pl.BlockSpec((None, N_META, d), lambda bi, i: (bi, 0, 0)),
                  _resident((1, d)), _resident(w_in.shape), _resident(w_grp.shape), _resident((1, d)),
                  _resident(w_out.shape)],
        out_specs=pl.BlockSpec((None, tm, d), lambda bi, i: (bi, i, 0)),
        scratch_shapes=[pltpu.VMEM((span, d), BF16), pltpu.VMEM((span + F32_SUBLANES, d), F32)]
                       + [pltpu.VMEM((span, d - k * gdim), F32) for k in range(1, n_groups)],
        compiler_params=_params(2),
        name="pool_mixer",
    )(h, h, h, h_meta, gain, w_in, w_grp, scale, w_out)


def _permute_heads(w, n_heads, axis):
    n_pairs = n_heads // (GQA_GROUP * HEADS_PER_LANE_GROUP)
    split = w.shape[:axis] + (n_pairs, HEADS_PER_LANE_GROUP, GQA_GROUP, HEAD_DIM) + w.shape[axis + 1:]
    return jnp.swapaxes(w.reshape(split), axis + 1, axis + 2).reshape(w.shape)


def _segment_mean_matrix():
    seg = np.kron(np.eye(MXU_DIM // HEAD_DIM), np.ones((HEAD_DIM, HEAD_DIM))) / HEAD_DIM
    return jnp.asarray(seg, dtype=BF16)


def kernel(x, meta_tokens, ffn_norm, w_gate_up, w_down, mixer_norm, w_qkv, q_norm, k_norm, sink_logit, w_o,
           w_pool_in, w_pool_group, pool_scale, w_pool_out):
    b, s, d = x.shape
    depth = ffn_norm.shape[0]
    n_heads = sink_logit.shape[1]
    dq = n_heads * HEAD_DIM
    dkv = dq // GQA_GROUP
    assert depth == 2 and s % TOKEN_TILE == 0 and d == dq
    seg = _segment_mean_matrix()

    hr = x.reshape(b * s, d)
    hm = jnp.broadcast_to(meta_tokens[None].astype(x.dtype), (b, N_META, d)).reshape(b * N_META, d)
    def ffn_weights(layer, which):
        return (w_gate_up, (layer, which)), (w_down, (layer, which))

    def gain(layer, which):
        return ffn_norm[layer, which][None]

    w00 = w_gate_up[0, 0].astype(BF16), w_down[0, 0].astype(BF16)
    w_qkv_p = jnp.concatenate([_permute_heads(w_qkv[0][:, :dq], n_heads, axis=1), w_qkv[0][:, dq:]],
                              axis=1).astype(BF16)
    w_o_p = _permute_heads(w_o[0], n_heads, axis=0).astype(BF16)
    qkv_args = (mixer_norm[0][None], w_qkv_p, jnp.tile(q_norm[0], n_heads)[None],
                jnp.tile(k_norm[0], dkv // HEAD_DIM)[None], seg)

    hr, q, k, vt, *w01 = _stage(hr, gain(0, 0), *w00, qkv=qkv_args, cast=ffn_weights(0, 1))
    hm, qm, km, vmt = _stage(hm, gain(0, 0), *w00, qkv=qkv_args)
    vm = vmt.transpose(0, 2, 1).reshape(b, N_META, dkv)
    o_real, o_meta = _attention(sink_logit[0], q.reshape(b, s, dq), k.reshape(b, s, dkv), vt,
                                qm.reshape(b, N_META, dq), km.reshape(b, N_META, dkv), vm, vm.transpose(0, 2, 1))
    hr, *w10 = _stage(hr, gain(0, 1), *w01, att=o_real.reshape(b * s, dq), wo=w_o_p, cast=ffn_weights(1, 0))
    (hm,) = _stage(hm, gain(0, 1), *w01, att=o_meta.reshape(b * N_META, dq), wo=w_o_p)

    n_grp, gdim = w_pool_group.shape[1:3]
    pool_weights = ((w_pool_in, (0,)), (w_pool_group.reshape(1, n_grp * gdim, gdim), (0,)), (w_pool_out, (0,)))
    hr, *casts = _stage(hr, gain(1, 0), *w10, cast=ffn_weights(1, 1) + pool_weights)
    (hm,) = _stage(hm, gain(1, 0), *w10)
    w11, (w_in, w_grp, w_out) = casts[:2], casts[2:]
    hr = _pool_mixer(hr.reshape(b, s, d), hm.reshape(b, N_META, d), mixer_norm[1][None], w_in,
                     w_grp.reshape(n_grp, gdim, gdim), pool_scale[0][None], w_out)
    (hr,) = _stage(hr.reshape(b * s, d), gain(1, 1), *w11)
    return hr.reshape(b, s, d)
```

```python
import functools

import numpy as np
import jax
import jax.numpy as jnp
from jax import lax
from jax.experimental import pallas as pl
from jax.experimental.pallas import tpu as pltpu

F32 = jnp.float32
BF16 = jnp.bfloat16

N_META = 16
HEAD_DIM = 64
GQA_GROUP = 4
WINDOW = 128
BLOCK = 128
POOL_WINDOWS = (2, 4, 8, 16)
RMS_EPS = 1e-6
NEG_INF = -1e30
LOG2E = 1.4426950408889634

LANES = 128
F32_SUBLANES = 8
BF16_SUBLANES = 16
MXU_DIM = 256
HEADS_PER_LANE_GROUP = LANES // HEAD_DIM
BAND = 3 * BLOCK
POOL_HALO = 16

TOKEN_TILE = 1024
QKV_TILE = 512
POOL_TILE = 512
ATTN_BLOCKS_PER_STEP = 4
SAFE_LOGIT_BOUND = 40.0
ROUNDING_MARGIN = 1.02
FF_CHUNK = 256
VMEM_LIMIT = 56 * 1024 * 1024


def _params(n_axes, vmem=VMEM_LIMIT):
    return pltpu.CompilerParams(dimension_semantics=("arbitrary",) * n_axes, vmem_limit_bytes=vmem)


def _resident(shape):
    nd = len(shape)
    return pl.BlockSpec(shape, lambda *_: (0,) * nd, pipeline_mode=pl.Buffered(1))


def _rows(tm, d):
    return pl.BlockSpec((tm, d), lambda i: (i, 0))


def _rms_norm(x, gain):
    return x * lax.rsqrt(jnp.mean(x * x, axis=-1, keepdims=True) + RMS_EPS) * gain


def _dot(a, b):
    return jnp.dot(a, b, preferred_element_type=F32)


def _dot_nt(a, b):
    return lax.dot_general(a, b, (((1,), (1,)), ((), ())), preferred_element_type=F32)


def _head_mean_square(t, seg_ref):
    sq = t * t
    hi = sq.astype(BF16)
    lo = (sq - hi.astype(F32)).astype(BF16)
    seg = seg_ref[...]
    cols = []
    for c in range(t.shape[1] // MXU_DIM):
        sl = slice(c * MXU_DIM, (c + 1) * MXU_DIM)
        cols.append(_dot(hi[:, sl], seg) + _dot(lo[:, sl], seg))
    return cols[0] if len(cols) == 1 else jnp.concatenate(cols, axis=1)


def _project_qkv(h, gain_ref, w_ref, qgain_ref, kgain_ref, seg_ref, q_ref, k_ref, vt_ref):
    dq, dkv = q_ref.shape[1], k_ref.shape[1]
    hn = _rms_norm(h, gain_ref[...]).astype(BF16)
    qkv = _dot(hn, w_ref[...])
    q = qkv[:, :dq]
    k = qkv[:, dq:dq + dkv]
    v = qkv[:, dq + dkv:]
    q = q * lax.rsqrt(_head_mean_square(q, seg_ref) + RMS_EPS) * qgain_ref[...] * (HEAD_DIM ** -0.5 * LOG2E)
    k = k * lax.rsqrt(_head_mean_square(k, seg_ref) + RMS_EPS) * kgain_ref[...]
    q_ref[...] = q.astype(q_ref.dtype)
    k_ref[...] = k.astype(k_ref.dtype)
    for t in range(vt_ref.shape[0]):
        vt_ref[t] = v[t * BLOCK:(t + 1) * BLOCK, :].T.astype(vt_ref.dtype)


def _stage_kernel(*refs, d_ff, chunk, pre_proj, post_qkv, n_cast):
    refs = list(refs)

    def take(n):
        taken, refs[:] = refs[:n], refs[n:]
        return taken

    (x_ref,) = take(1)
    att_ref, wo_ref = take(2) if pre_proj else (None, None)
    gain_ref, wgu_ref, wd_ref = take(3)
    qkv_in = take(5) if post_qkv else []
    cast_in = take(n_cast)
    (out_ref,) = take(1)
    qkv_out = take(3) if post_qkv else []
    cast_out = take(n_cast)
    (act_ref,) = take(1)

    x = x_ref[...]
    if pre_proj:
        x = x + _dot(att_ref[...], wo_ref[...])
    xn = _rms_norm(x, gain_ref[...]).astype(BF16)
    for c in range(d_ff // chunk):
        gate = _dot(xn, wgu_ref[:, c * chunk:(c + 1) * chunk])
        up = _dot(xn, wgu_ref[:, d_ff + c * chunk:d_ff + (c + 1) * chunk])
        act_ref[:, c * chunk:(c + 1) * chunk] = (jax.nn.silu(gate) * up).astype(BF16)
    h = x + 0.5 * _dot(act_ref[...], wd_ref[...])
    out_ref[...] = h
    if post_qkv:
        _project_qkv(h, *qkv_in, *qkv_out)
    for src, dst in zip(cast_in, cast_out):
        dst[...] = src[...].astype(dst.dtype)


def _cast_specs(rows, cols, prefix, steps):
    block = next(r for r in range(BF16_SUBLANES, rows + 1, BF16_SUBLANES) if rows % r == 0 and r * steps >= rows)
    last = rows // block - 1
    in_spec = pl.BlockSpec((None,) * len(prefix) + (block, cols), lambda i: (*prefix, jnp.minimum(i, last), 0))
    return in_spec, pl.BlockSpec((block, cols), lambda i: (jnp.minimum(i, last), 0))


def _stage(h, gain, wgu, wd, att=None, wo=None, qkv=None, cast=()):
    t, d = h.shape
    d_ff = wd.shape[0]
    tm = min(TOKEN_TILE if qkv is None else QKV_TILE, t)
    steps = t // tm
    operands, in_specs = [h], [_rows(tm, d)]
    if att is not None:
        operands += [att, wo]
        in_specs += [_rows(tm, att.shape[1]), _resident(wo.shape)]
    operands += [gain, wgu, wd]
    in_specs += [_resident(gain.shape), _resident(wgu.shape), _resident(wd.shape)]
    out_shape, out_specs = [jax.ShapeDtypeStruct((t, d), h.dtype)], [_rows(tm, d)]
    if qkv is not None:
        operands += list(qkv)
        in_specs += [_resident(a.shape) for a in qkv]
        dq, dkv = qkv[2].shape[1], qkv[3].shape[1]
        out_shape += [jax.ShapeDtypeStruct((t, dq), BF16), jax.ShapeDtypeStruct((t, dkv), BF16),
                      jax.ShapeDtypeStruct((t // BLOCK, dkv, BLOCK), BF16)]
        out_specs += [_rows(tm, dq), _rows(tm, dkv), pl.BlockSpec((tm // BLOCK, dkv, BLOCK), lambda i: (i, 0, 0))]
    for w, prefix in cast:
        in_spec, out_spec = _cast_specs(w.shape[-2], w.shape[-1], prefix, steps)
        operands.append(w)
        in_specs.append(in_spec)
        out_shape.append(jax.ShapeDtypeStruct(w.shape[-2:], BF16))
        out_specs.append(out_spec)
    return pl.pallas_call(
        functools.partial(_stage_kernel, d_ff=d_ff, chunk=FF_CHUNK, pre_proj=att is not None,
                          post_qkv=qkv is not None, n_cast=len(cast)),
        out_shape=out_shape,
        grid=(steps,),
        in_specs=in_specs,
        out_specs=out_specs,
        scratch_shapes=[pltpu.VMEM((tm, d_ff), BF16)],
        compiler_params=_params(1),
        name="stage",
    )(*operands)


def _slope2(head, n_heads):
    return 2.0 ** (-8.0 * (head + 1) / n_heads) * LOG2E


def _head_index(j, e, g):
    return (HEADS_PER_LANE_GROUP * j + e) * GQA_GROUP + g


def _stack_queries(q, j):
    rows = q.shape[0]
    low = lax.broadcasted_iota(jnp.int32, (rows, LANES), 1) < HEAD_DIM
    zero = jnp.zeros((rows, LANES), q.dtype)
    groups = [q[:, (j * GQA_GROUP + g) * LANES:(j * GQA_GROUP + g + 1) * LANES] for g in range(GQA_GROUP)]
    return jnp.concatenate([jnp.where(low, qg, zero) for qg in groups]
                           + [jnp.where(low, zero, qg) for qg in groups], axis=0)


def _attend_block(n, last, row0, exact, sink_ref, shift_ref, q_ref, k_ref, vt_ref, km_ref, vmt_ref, o_ref,
                  bias_l_ref, bias_c_ref, bias_r_ref, bias_m_ref, vall_ref, p_ref):
    n_heads = q_ref.shape[1] // HEAD_DIM
    left = jnp.maximum(n - 1, 0)
    right = jnp.minimum(n + 1, last)
    no_left = (n == 0).astype(jnp.int32)
    no_right = (n == last).astype(jnp.int32)
    blocks = (left, n, right)
    keys = jnp.concatenate([k_ref[pl.ds(pl.multiple_of(blk * BLOCK, BLOCK), BLOCK), :] for blk in blocks]
                           + [km_ref[...]], axis=0)
    for i, blk in enumerate(blocks):
        vall_ref[:, i * BLOCK:(i + 1) * BLOCK] = vt_ref[blk]
    vall_ref[:, BAND:BAND + N_META] = vmt_ref[...]
    q = q_ref[pl.ds(row0, BLOCK), :]
    block_shift = (n * BLOCK).astype(F32)
    upper = lax.broadcasted_iota(jnp.int32, (LANES, BLOCK), 0) < HEAD_DIM

    low = lax.broadcasted_iota(jnp.int32, (BLOCK, LANES), 1) < HEAD_DIM
    zero = jnp.zeros((BLOCK, LANES), q.dtype)
    heads_per_dot = MXU_DIM // BLOCK
    for j in range(keys.shape[1] // LANES):
        kg = keys[:, j * LANES:(j + 1) * LANES]
        inv = []
        for e in range(HEADS_PER_LANE_GROUP):
            for g0 in range(0, GQA_GROUP, heads_per_dot):
                groups = [q[:, (j * GQA_GROUP + g) * LANES:(j * GQA_GROUP + g + 1) * LANES]
                          for g in range(g0, g0 + heads_per_dot)]
                halves = [jnp.where(low, qg, zero) if e == 0 else jnp.where(low, zero, qg) for qg in groups]
                st = _dot_nt(kg, jnp.concatenate(halves, axis=0))
                for gi in range(heads_per_dot):
                    head = _head_index(j, e, g0 + gi)
                    src = slice(gi * BLOCK, (gi + 1) * BLOCK)
                    cols = slice((e * GQA_GROUP + g0 + gi) * BLOCK, (e * GQA_GROUP + g0 + gi + 1) * BLOCK)
                    s_l = st[0:BLOCK, src] + bias_l_ref[no_left, head]
                    s_c = st[BLOCK:2 * BLOCK, src] + bias_c_ref[head]
                    s_r = st[2 * BLOCK:BAND, src] + bias_r_ref[no_right, head]
                    s_m = st[BAND:BAND + N_META, src] + (bias_m_ref[head] - _slope2(head, n_heads) * block_shift)
                    sink = sink_ref[head] * LOG2E - shift_ref[head]
                    if exact:
                        m = jnp.max(jnp.maximum(jnp.maximum(s_l, s_c), s_r), axis=0, keepdims=True)
                        m = jnp.maximum(jnp.maximum(m, jnp.max(s_m, axis=0, keepdims=True)), sink)
                        s_l, s_c, s_r, s_m = s_l - m, s_c - m, s_r - m, s_m - m
                        sink_term = jnp.exp2(sink - m)
                    else:
                        sink_term = jnp.exp2(jnp.full((1, BLOCK), sink, F32))
                    e_l, e_c, e_r, e_m = jnp.exp2(s_l), jnp.exp2(s_c), jnp.exp2(s_r), jnp.exp2(s_m)
                    denom = (jnp.sum(e_l + e_c + e_r, axis=0, keepdims=True) + jnp.sum(e_m, axis=0, keepdims=True)
                             + sink_term)
                    inv.append(1.0 / denom)
                    p_ref[j, 0:BLOCK, cols] = e_l.astype(p_ref.dtype)
                    p_ref[j, BLOCK:2 * BLOCK, cols] = e_c.astype(p_ref.dtype)
                    p_ref[j, 2 * BLOCK:BAND, cols] = e_r.astype(p_ref.dtype)
                    p_ref[j, BAND:BAND + N_META, cols] = e_m.astype(p_ref.dtype)
        out_t = _dot(vall_ref[j * LANES:(j + 1) * LANES, :], p_ref[j])
        for g in range(GQA_GROUP):
            c0, c1 = g * BLOCK, (GQA_GROUP + g) * BLOCK
            og_t = jnp.where(upper, out_t[:, c0:c0 + BLOCK] * inv[g], out_t[:, c1:c1 + BLOCK] * inv[GQA_GROUP + g])
            lanes = slice((j * GQA_GROUP + g) * LANES, (j * GQA_GROUP + g + 1) * LANES)
            o_ref[pl.ds(row0, BLOCK), lanes] = og_t.T.astype(o_ref.dtype)


def _attn_real_kernel(exact_ref, sink_ref, shift_ref, q_ref, k_ref, vt_ref, km_ref, vmt_ref, o_ref,
                      bias_l_ref, bias_c_ref, bias_r_ref, bias_m_ref, vall_ref, p_ref):
    first = (pl.program_id(0) == 0) & (pl.program_id(1) == 0)
    n_heads = q_ref.shape[1] // HEAD_DIM
    per_step = q_ref.shape[0] // BLOCK
    last = pl.num_programs(1) * per_step - 1

    @pl.when(first)
    def _init_tables():
        key = lax.broadcasted_iota(jnp.int32, (BLOCK, BLOCK), 0)
        qry = lax.broadcasted_iota(jnp.int32, (BLOCK, BLOCK), 1)
        masked = jnp.full((BLOCK, BLOCK), NEG_INF, F32)
        meta = lax.broadcasted_iota(jnp.int32, (N_META, BLOCK), 0)
        mqry = lax.broadcasted_iota(jnp.int32, (N_META, BLOCK), 1)
        for h in range(n_heads):
            slope = _slope2(h, n_heads)
            shift = shift_ref[h]
            for ref, delta in ((bias_l_ref, key - BLOCK - qry), (bias_c_ref, key - qry), (bias_r_ref, key + BLOCK - qry)):
                dist = jnp.abs(delta)
                table = jnp.where(dist <= WINDOW, -slope * dist.astype(F32) - shift, NEG_INF)
                if ref is bias_c_ref:
                    ref[h] = table
                else:
                    ref[0, h] = table
                    ref[1, h] = masked
            bias_m_ref[h] = -slope * (N_META + mqry - meta).astype(F32) - shift
        vall_ref[...] = jnp.zeros_like(vall_ref)
        p_ref[...] = jnp.zeros_like(p_ref)

    refs = (sink_ref, shift_ref, q_ref, k_ref, vt_ref, km_ref, vmt_ref, o_ref,
            bias_l_ref, bias_c_ref, bias_r_ref, bias_m_ref, vall_ref, p_ref)
    needs_max = exact_ref[0] != 0

    def block(sub, carry):
        n = pl.program_id(1) * per_step + sub
        row0 = pl.multiple_of(sub * BLOCK, BLOCK)

        @pl.when(needs_max)
        def _exact():
            _attend_block(n, last, row0, True, *refs)

        @pl.when(jnp.logical_not(needs_max))
        def _bounded():
            _attend_block(n, last, row0, False, *refs)

        return carry

    lax.fori_loop(0, per_step, block, 0)


def _attn_meta_kernel(sink_ref, q_ref, k_ref, vt_ref, km_ref, vm_ref, o_ref):
    rows = q_ref.shape[0]
    n_heads = q_ref.shape[1] // HEAD_DIM
    keys = jnp.concatenate([k_ref[...], km_ref[...]], axis=0)
    vals = jnp.concatenate([vt_ref[0].astype(F32).T.astype(BF16), vm_ref[...]], axis=0)
    nk = BLOCK + N_META
    qpos = lax.broadcasted_iota(jnp.int32, (rows, nk), 0)
    col = lax.broadcasted_iota(jnp.int32, (rows, nk), 1)
    idist = jnp.abs(qpos - jnp.where(col < BLOCK, N_META + col, col - BLOCK))
    valid = idist <= WINDOW
    dist = idist.astype(F32)
    low = lax.broadcasted_iota(jnp.int32, (rows, LANES), 1) < HEAD_DIM
    q = q_ref[...]
    for j in range(keys.shape[1] // LANES):
        scores = _dot_nt(_stack_queries(q, j), keys[:, j * LANES:(j + 1) * LANES])
        probs = []
        for e in range(HEADS_PER_LANE_GROUP):
            for g in range(GQA_GROUP):
                head = _head_index(j, e, g)
                r0 = (e * GQA_GROUP + g) * rows
                s = jnp.where(valid, scores[r0:r0 + rows] - _slope2(head, n_heads) * dist, NEG_INF)
                sink = sink_ref[head] * LOG2E
                m = jnp.maximum(jnp.max(s, axis=-1, keepdims=True), sink)
                ex = jnp.exp2(s - m)
                denom = jnp.sum(ex, axis=-1, keepdims=True) + jnp.exp2(sink - m)
                probs.append((ex * (1.0 / denom)).astype(BF16))
        out = _dot(jnp.concatenate(probs, axis=0), vals[:, j * LANES:(j + 1) * LANES])
        for g in range(GQA_GROUP):
            og = jnp.where(low, out[g * rows:(g + 1) * rows], out[(GQA_GROUP + g) * rows:(GQA_GROUP + g + 1) * rows])
            lanes = slice((j * GQA_GROUP + g) * LANES, (j * GQA_GROUP + g + 1) * LANES)
            o_ref[:, lanes] = og.astype(o_ref.dtype)


def _attention(sink, logit_bound, q, k, vt, qm, km, vm, vmt):
    b, s, dq = q.shape
    dkv = k.shape[2]
    nb = s // BLOCK
    n_heads = dq // HEAD_DIM
    n_pairs = dkv // LANES
    padded_keys = 2 * MXU_DIM
    rows = ATTN_BLOCKS_PER_STEP * BLOCK
    exact = logit_bound > SAFE_LOGIT_BOUND
    shift = jnp.where(exact, 0.0, jnp.maximum(logit_bound, sink * LOG2E))
    smem = pl.BlockSpec(memory_space=pltpu.SMEM)
    o_real = pl.pallas_call(
        _attn_real_kernel,
        out_shape=jax.ShapeDtypeStruct((b, s, dq), BF16),
        grid=(b, s // rows),
        in_specs=[smem, smem, smem,
                  pl.BlockSpec((None, rows, dq), lambda i, n: (i, n, 0)),
                  pl.BlockSpec((None, s, dkv), lambda i, n: (i, 0, 0)),
                  pl.BlockSpec((nb, dkv, BLOCK), lambda i, n: (i, 0, 0)),
                  pl.BlockSpec((None, N_META, dkv), lambda i, n: (i, 0, 0)),
                  pl.BlockSpec((None, dkv, N_META), lambda i, n: (i, 0, 0))],
        out_specs=pl.BlockSpec((None, rows, dq), lambda i, n: (i, n, 0)),
        scratch_shapes=[pltpu.VMEM((2, n_heads, BLOCK, BLOCK), F32), pltpu.VMEM((n_heads, BLOCK, BLOCK), F32),
                        pltpu.VMEM((2, n_heads, BLOCK, BLOCK), F32), pltpu.VMEM((n_heads, N_META, BLOCK), F32),
                        pltpu.VMEM((dkv, padded_keys), BF16),
                        pltpu.VMEM((n_pairs, padded_keys, 2 * GQA_GROUP * BLOCK), BF16)],
        compiler_params=_params(2),
        name="attn_real",
    )(exact.astype(jnp.int32)[None], sink, shift.astype(F32), q, k, vt, km, vmt)
    o_meta = pl.pallas_call(
        _attn_meta_kernel,
        out_shape=jax.ShapeDtypeStruct((b, N_META, dq), BF16),
        grid=(b,),
        in_specs=[smem, pl.BlockSpec((None, N_META, dq), lambda i: (i, 0, 0)),
                  pl.BlockSpec((None, BLOCK, dkv), lambda i: (i, 0, 0)),
                  pl.BlockSpec((1, dkv, BLOCK), lambda i: (i * nb, 0, 0)),
                  pl.BlockSpec((None, N_META, dkv), lambda i: (i, 0, 0)),
                  pl.BlockSpec((None, N_META, dkv), lambda i: (i, 0, 0))],
        out_specs=pl.BlockSpec((None, N_META, dq), lambda i: (i, 0, 0)),
        compiler_params=_params(1),
        name="attn_meta",
    )(sink, qm, k, vt, km, vm)
    return o_real, o_meta


def _pool_kernel(h_ref, prev_ref, next_ref, meta_ref, gain_ref, win_ref, wgrp_ref, scale_ref, wout_ref, o_ref,
                 hn_ref, u_ref, *run_refs, tm, total_len):
    i = pl.program_id(1)
    last = pl.num_programs(1) - 1
    gain = gain_ref[...]
    before = jnp.where(i == 0, meta_ref[...], prev_ref[...])
    hn_ref[0:POOL_HALO, :] = _rms_norm(before, gain).astype(BF16)
    hn_ref[POOL_HALO:POOL_HALO + tm, :] = _rms_norm(h_ref[...], gain).astype(BF16)
    after = _rms_norm(next_ref[...], gain)
    hn_ref[POOL_HALO + tm:, :] = jnp.where(i == last, 0.0, after).astype(BF16)
    span = tm + 2 * POOL_HALO
    u_ref[0:span, :] = _dot(hn_ref[...], win_ref[...])
    u_ref[span:, :] = jnp.zeros((u_ref.shape[0] - span, u_ref.shape[1]), F32)

    gdim = u_ref.shape[1] // len(POOL_WINDOWS)
    runs = {1: (u_ref, 0)}
    src_ref, src_col, width = u_ref, 0, 1
    for k, run_ref in enumerate(run_refs, start=1):
        col = k * gdim
        rows = span - F32_SUBLANES * k
        lo = F32_SUBLANES
        run_ref[lo:lo + rows, :] = (src_ref[lo:lo + rows, col - src_col:]
                                    + src_ref[lo + width:lo + width + rows, col - src_col:])
        src_ref, src_col, width = run_ref, col, 2 * width
        runs[width] = (run_ref, col)

    pos = N_META + i * tm + lax.broadcasted_iota(jnp.int32, (tm, 1), 0)
    mixed = []
    for g, window in enumerate(POOL_WINDOWS):
        half = window // 2
        run_ref, col = runs[half]
        cols = slice(g * gdim - col, (g + 1) * gdim - col)
        total = run_ref[POOL_HALO - half:POOL_HALO - half + tm, cols] + run_ref[POOL_HALO:POOL_HALO + tm, cols]
        count = (half + jnp.minimum(half, total_len - pos)).astype(F32)
        pooled = total / count - u_ref[POOL_HALO:POOL_HALO + tm, g * gdim:(g + 1) * gdim]
        mixed.append(_dot(pooled.astype(BF16), wgrp_ref[g]))
    y = (jnp.concatenate(mixed, axis=1) * scale_ref[...]).astype(BF16)
    o_ref[...] = h_ref[...] + _dot(y, wout_ref[...])


def _pool_mixer(h, h_meta, gain, w_in, w_grp, scale, w_out):
    b, s, d = h.shape
    tm = POOL_TILE
    per_tile = tm // POOL_HALO
    n_halo_blocks = s // POOL_HALO
    span = tm + 2 * POOL_HALO
    n_groups = len(POOL_WINDOWS)
    gdim = d // n_groups
    assert POOL_WINDOWS == tuple(2 ** (g + 1) for g in range(n_groups)) and POOL_WINDOWS[-1] // 2 <= F32_SUBLANES
    return pl.pallas_call(
        functools.partial(_pool_kernel, tm=tm, total_len=N_META + s),
        out_shape=jax.ShapeDtypeStruct((b, s, d), h.dtype),
        grid=(b, s // tm),
        in_specs=[pl.BlockSpec((None, tm, d), lambda bi, i: (bi, i, 0)),
                  pl.BlockSpec((None, POOL_HALO, d), lambda bi, i: (bi, jnp.maximum(i * per_tile - 1, 0), 0)),
                  pl.BlockSpec((None, POOL_HALO, d),
                               lambda bi, i: (bi, jnp.minimum((i + 1) * per_tile, n_halo_blocks - 1), 0)),
                  pl.BlockSpec((None, N_META, d), lambda bi, i: (bi, 0, 0)),
                  _resident((1, d)), _resident(w_in.shape), _resident(w_grp.shape), _resident((1, d)),
                  _resident(w_out.shape)],
        out_specs=pl.BlockSpec((None, tm, d), lambda bi, i: (bi, i, 0)),
        scratch_shapes=[pltpu.VMEM((span, d), BF16), pltpu.VMEM((span + F32_SUBLANES, d), F32)]
                       + [pltpu.VMEM((span, d - k * gdim), F32) for k in range(1, n_groups)],
        compiler_params=_params(2),
        name="pool_mixer",
    )(h, h, h, h_meta, gain, w_in, w_grp, scale, w_out)


def _permute_heads(w, n_heads, axis):
    n_pairs = n_heads // (GQA_GROUP * HEADS_PER_LANE_GROUP)
    split = w.shape[:axis] + (n_pairs, HEADS_PER_LANE_GROUP, GQA_GROUP, HEAD_DIM) + w.shape[axis + 1:]
    return jnp.swapaxes(w.reshape(split), axis + 1, axis + 2).reshape(w.shape)


def _segment_mean_matrix():
    seg = np.kron(np.eye(MXU_DIM // HEAD_DIM), np.ones((HEAD_DIM, HEAD_DIM))) / HEAD_DIM
    return jnp.asarray(seg, dtype=BF16)


def kernel(x, meta_tokens, ffn_norm, w_gate_up, w_down, mixer_norm, w_qkv, q_norm, k_norm, sink_logit, w_o,
           w_pool_in, w_pool_group, pool_scale, w_pool_out):
    b, s, d = x.shape
    depth = ffn_norm.shape[0]
    n_heads = sink_logit.shape[1]
    dq = n_heads * HEAD_DIM
    dkv = dq // GQA_GROUP
    assert depth == 2 and s % TOKEN_TILE == 0 and d == dq
    seg = _segment_mean_matrix()

    hr = x.reshape(b * s, d)
    hm = jnp.broadcast_to(meta_tokens[None].astype(x.dtype), (b, N_META, d)).reshape(b * N_META, d)
    def ffn_weights(layer, which):
        return (w_gate_up, (layer, which)), (w_down, (layer, which))

    def gain(layer, which):
        return ffn_norm[layer, which][None]

    w00 = w_gate_up[0, 0].astype(BF16), w_down[0, 0].astype(BF16)
    w_qkv_p = jnp.concatenate([_permute_heads(w_qkv[0][:, :dq], n_heads, axis=1), w_qkv[0][:, dq:]],
                              axis=1).astype(BF16)
    w_o_p = _permute_heads(w_o[0], n_heads, axis=0).astype(BF16)
    qkv_args = (mixer_norm[0][None], w_qkv_p, jnp.tile(q_norm[0], n_heads)[None],
                jnp.tile(k_norm[0], dkv // HEAD_DIM)[None], seg)

    hr, q, k, vt, *w01 = _stage(hr, gain(0, 0), *w00, qkv=qkv_args, cast=ffn_weights(0, 1))
    hm, qm, km, vmt = _stage(hm, gain(0, 0), *w00, qkv=qkv_args)
    vm = vmt.transpose(0, 2, 1).reshape(b, N_META, dkv)
    logit_bound = (HEAD_DIM ** 0.5 * LOG2E * ROUNDING_MARGIN) * jnp.max(jnp.abs(q_norm[0])) * jnp.max(jnp.abs(k_norm[0]))
    o_real, o_meta = _attention(sink_logit[0], logit_bound, q.reshape(b, s, dq), k.reshape(b, s, dkv), vt,
                                qm.reshape(b, N_META, dq), km.reshape(b, N_META, dkv), vm, vm.transpose(0, 2, 1))
    hr, *w10 = _stage(hr, gain(0, 1), *w01, att=o_real.reshape(b * s, dq), wo=w_o_p, cast=ffn_weights(1, 0))
    (hm,) = _stage(hm, gain(0, 1), *w01, att=o_meta.reshape(b * N_META, dq), wo=w_o_p)

    n_grp, gdim = w_pool_group.shape[1:3]
    pool_weights = ((w_pool_in, (0,)), (w_pool_group.reshape(1, n_grp * gdim, gdim), (0,)), (w_pool_out, (0,)))
    hr, *casts = _stage(hr, gain(1, 0), *w10, cast=ffn_weights(1, 1) + pool_weights)
    (hm,) = _stage(hm, gain(1, 0), *w10)
    w11, (w_in, w_grp, w_out) = casts[:2], casts[2:]
    hr = _pool_mixer(hr.reshape(b, s, d), hm.reshape(b, N_META, d), mixer_norm[1][None], w_in,
                     w_grp.reshape(n_grp, gdim, gdim), pool_scale[0][None], w_out)
    (hr,) = _stage(hr.reshape(b * s, d), gain(1, 1), *w11)
    return hr.reshape(b, s, d)
```

```python
import functools

import numpy as np
import jax
import jax.numpy as jnp
from jax import lax
from jax.experimental import pallas as pl
from jax.experimental.pallas import tpu as pltpu

F32 = jnp.float32
BF16 = jnp.bfloat16

N_META = 16
HEAD_DIM = 64
GQA_GROUP = 4
WINDOW = 128
BLOCK = 128
POOL_WINDOWS = (2, 4, 8, 16)
RMS_EPS = 1e-6
NEG_INF = -1e30
LOG2E = 1.4426950408889634

LANES = 128
F32_SUBLANES = 8
BF16_SUBLANES = 16
MXU_DIM = 256
HEADS_PER_LANE_GROUP = LANES // HEAD_DIM
BAND = 3 * BLOCK
POOL_HALO = 16

TOKEN_TILE = 1024
QKV_TILE = 512
POOL_TILE = 1024
ATTN_BLOCKS_PER_STEP = 16
SAFE_LOGIT_BOUND = 40.0
ROUNDING_MARGIN = 1.02
FF_CHUNK = 256
VMEM_LIMIT = 56 * 1024 * 1024


def _params(n_axes, vmem=VMEM_LIMIT):
    return pltpu.CompilerParams(dimension_semantics=("arbitrary",) * n_axes, vmem_limit_bytes=vmem)


def _resident(shape):
    nd = len(shape)
    return pl.BlockSpec(shape, lambda *_: (0,) * nd, pipeline_mode=pl.Buffered(1))


def _rows(tm, d):
    return pl.BlockSpec((tm, d), lambda i: (i, 0))


def _rms_norm(x, gain):
    return x * lax.rsqrt(jnp.mean(x * x, axis=-1, keepdims=True) + RMS_EPS) * gain


def _dot(a, b):
    return jnp.dot(a, b, preferred_element_type=F32)


def _dot_nt(a, b):
    return lax.dot_general(a, b, (((1,), (1,)), ((), ())), preferred_element_type=F32)


def _head_mean_square(t, seg_ref):
    sq = (t * t).astype(BF16)
    seg = seg_ref[...]
    cols = [_dot(sq[:, c * MXU_DIM:(c + 1) * MXU_DIM], seg) for c in range(t.shape[1] // MXU_DIM)]
    return cols[0] if len(cols) == 1 else jnp.concatenate(cols, axis=1)


def _project_qkv(h, gain_ref, w_ref, qgain_ref, kgain_ref, seg_ref, q_ref, k_ref, vt_ref):
    dq, dkv = q_ref.shape[1], k_ref.shape[1]
    hn = _rms_norm(h, gain_ref[...]).astype(BF16)
    qkv = _dot(hn, w_ref[...])
    q = qkv[:, :dq]
    k = qkv[:, dq:dq + dkv]
    v = qkv[:, dq + dkv:]
    q = q * lax.rsqrt(_head_mean_square(q, seg_ref) + RMS_EPS) * qgain_ref[...] * (HEAD_DIM ** -0.5 * LOG2E)
    k = k * lax.rsqrt(_head_mean_square(k, seg_ref) + RMS_EPS) * kgain_ref[...]
    q_ref[...] = q.astype(q_ref.dtype)
    k_ref[...] = k.astype(k_ref.dtype)
    for t in range(vt_ref.shape[0]):
        vt_ref[t] = v[t * BLOCK:(t + 1) * BLOCK, :].T.astype(vt_ref.dtype)


def _stage_kernel(*refs, d_ff, chunk, pre_proj, post_qkv, n_cast):
    refs = list(refs)

    def take(n):
        taken, refs[:] = refs[:n], refs[n:]
        return taken

    (x_ref,) = take(1)
    att_ref, wo_ref = take(2) if pre_proj else (None, None)
    gain_ref, wgu_ref, wd_ref = take(3)
    qkv_in = take(5) if post_qkv else []
    cast_in = take(n_cast)
    (out_ref,) = take(1)
    qkv_out = take(3) if post_qkv else []
    cast_out = take(n_cast)
    (act_ref,) = take(1)

    x = x_ref[...]
    if pre_proj:
        x = x + _dot(att_ref[...], wo_ref[...])
    xn = _rms_norm(x, gain_ref[...]).astype(BF16)
    for c in range(d_ff // chunk):
        gate = _dot(xn, wgu_ref[:, c * chunk:(c + 1) * chunk])
        up = _dot(xn, wgu_ref[:, d_ff + c * chunk:d_ff + (c + 1) * chunk])
        act_ref[:, c * chunk:(c + 1) * chunk] = (jax.nn.silu(gate) * up).astype(BF16)
    h = x + 0.5 * _dot(act_ref[...], wd_ref[...])
    out_ref[...] = h
    if post_qkv:
        _project_qkv(h, *qkv_in, *qkv_out)
    for src, dst in zip(cast_in, cast_out):
        dst[...] = src[...].astype(dst.dtype)


def _cast_specs(rows, cols, prefix, steps):
    block = next(r for r in range(BF16_SUBLANES, rows + 1, BF16_SUBLANES) if rows % r == 0 and r * steps >= rows)
    last = rows // block - 1
    in_spec = pl.BlockSpec((None,) * len(prefix) + (block, cols), lambda i: (*prefix, jnp.minimum(i, last), 0))
    return in_spec, pl.BlockSpec((block, cols), lambda i: (jnp.minimum(i, last), 0))


def _stage(h, gain, wgu, wd, att=None, wo=None, qkv=None, cast=()):
    t, d = h.shape
    d_ff = wd.shape[0]
    tm = min(TOKEN_TILE if qkv is None else QKV_TILE, t)
    steps = t // tm
    operands, in_specs = [h], [_rows(tm, d)]
    if att is not None:
        operands += [att, wo]
        in_specs += [_rows(tm, att.shape[1]), _resident(wo.shape)]
    operands += [gain, wgu, wd]
    in_specs += [_resident(gain.shape), _resident(wgu.shape), _resident(wd.shape)]
    out_shape, out_specs = [jax.ShapeDtypeStruct((t, d), h.dtype)], [_rows(tm, d)]
    if qkv is not None:
        operands += list(qkv)
        in_specs += [_resident(a.shape) for a in qkv]
        dq, dkv = qkv[2].shape[1], qkv[3].shape[1]
        out_shape += [jax.ShapeDtypeStruct((t, dq), BF16), jax.ShapeDtypeStruct((t, dkv), BF16),
                      jax.ShapeDtypeStruct((t // BLOCK, dkv, BLOCK), BF16)]
        out_specs += [_rows(tm, dq), _rows(tm, dkv), pl.BlockSpec((tm // BLOCK, dkv, BLOCK), lambda i: (i, 0, 0))]
    for w, prefix in cast:
        in_spec, out_spec = _cast_specs(w.shape[-2], w.shape[-1], prefix, steps)
        operands.append(w)
        in_specs.append(in_spec)
        out_shape.append(jax.ShapeDtypeStruct(w.shape[-2:], BF16))
        out_specs.append(out_spec)
    return pl.pallas_call(
        functools.partial(_stage_kernel, d_ff=d_ff, chunk=FF_CHUNK, pre_proj=att is not None,
                          post_qkv=qkv is not None, n_cast=len(cast)),
        out_shape=out_shape,
        grid=(steps,),
        in_specs=in_specs,
        out_specs=out_specs,
        scratch_shapes=[pltpu.VMEM((tm, d_ff), BF16)],
        compiler_params=_params(1),
        name="stage",
    )(*operands)


def _slope2(head, n_heads):
    return 2.0 ** (-8.0 * (head + 1) / n_heads) * LOG2E


def _head_index(j, e, g):
    return (HEADS_PER_LANE_GROUP * j + e) * GQA_GROUP + g


def _stack_queries(q, j):
    rows = q.shape[0]
    low = lax.broadcasted_iota(jnp.int32, (rows, LANES), 1) < HEAD_DIM
    zero = jnp.zeros((rows, LANES), q.dtype)
    groups = [q[:, (j * GQA_GROUP + g) * LANES:(j * GQA_GROUP + g + 1) * LANES] for g in range(GQA_GROUP)]
    return jnp.concatenate([jnp.where(low, qg, zero) for qg in groups]
                           + [jnp.where(low, zero, qg) for qg in groups], axis=0)


def _attend_block(n, last, row0, exact, sink_ref, shift_ref, q_ref, k_ref, vt_ref, km_ref, vmt_ref, o_ref,
                  bias_l_ref, bias_c_ref, bias_r_ref, bias_m_ref, vall_ref, p_ref):
    n_heads = q_ref.shape[1] // HEAD_DIM
    n_pairs = k_ref.shape[1] // LANES
    left = jnp.maximum(n - 1, 0)
    right = jnp.minimum(n + 1, last)
    no_left = (n == 0).astype(jnp.int32)
    no_right = (n == last).astype(jnp.int32)
    blocks = (left, n, right)
    keys = jnp.concatenate([k_ref[pl.ds(pl.multiple_of(blk * BLOCK, BLOCK), BLOCK), :] for blk in blocks]
                           + [km_ref[...]], axis=0)
    for i, blk in enumerate(blocks):
        vall_ref[:, i * BLOCK:(i + 1) * BLOCK] = vt_ref[blk]
    vall_ref[:, BAND:BAND + N_META] = vmt_ref[...]
    q = q_ref[pl.ds(row0, BLOCK), :]
    block_shift = (n * BLOCK).astype(F32)
    upper = lax.broadcasted_iota(jnp.int32, (LANES, BLOCK), 0) < HEAD_DIM
    low = lax.broadcasted_iota(jnp.int32, (BLOCK, LANES), 1) < HEAD_DIM
    zero = jnp.zeros((BLOCK, LANES), q.dtype)
    heads_per_dot = MXU_DIM // BLOCK

    inv = []
    for j in range(n_pairs):
        kg = keys[:, j * LANES:(j + 1) * LANES]
        for e in range(HEADS_PER_LANE_GROUP):
            for g0 in range(0, GQA_GROUP, heads_per_dot):
                groups = [q[:, (j * GQA_GROUP + g) * LANES:(j * GQA_GROUP + g + 1) * LANES]
                          for g in range(g0, g0 + heads_per_dot)]
                halves = [jnp.where(low, qg, zero) if e == 0 else jnp.where(low, zero, qg) for qg in groups]
                st = _dot_nt(kg, jnp.concatenate(halves, axis=0))
                for gi in range(heads_per_dot):
                    head = _head_index(j, e, g0 + gi)
                    src = slice(gi * BLOCK, (gi + 1) * BLOCK)
                    cols = slice((e * GQA_GROUP + g0 + gi) * BLOCK, (e * GQA_GROUP + g0 + gi + 1) * BLOCK)
                    s_l = st[0:BLOCK, src] + bias_l_ref[no_left, head]
                    s_c = st[BLOCK:2 * BLOCK, src] + bias_c_ref[head]
                    s_r = st[2 * BLOCK:BAND, src] + bias_r_ref[no_right, head]
                    s_m = st[BAND:BAND + N_META, src] + (bias_m_ref[head] - _slope2(head, n_heads) * block_shift)
                    sink = sink_ref[head] * LOG2E - shift_ref[head]
                    if exact:
                        m = jnp.max(jnp.maximum(jnp.maximum(s_l, s_c), s_r), axis=0, keepdims=True)
                        m = jnp.maximum(jnp.maximum(m, jnp.max(s_m, axis=0, keepdims=True)), sink)
                        s_l, s_c, s_r, s_m = s_l - m, s_c - m, s_r - m, s_m - m
                        sink_term = jnp.exp2(sink - m)
                    else:
                        sink_term = jnp.exp2(jnp.full((1, BLOCK), sink, F32))
                    e_l, e_c, e_r, e_m = jnp.exp2(s_l), jnp.exp2(s_c), jnp.exp2(s_r), jnp.exp2(s_m)
                    denom = (jnp.sum(e_l + e_c + e_r, axis=0, keepdims=True) + jnp.sum(e_m, axis=0, keepdims=True)
                             + sink_term)
                    inv.append(1.0 / denom)
                    p_ref[j, 0:BLOCK, cols] = e_l.astype(p_ref.dtype)
                    p_ref[j, BLOCK:2 * BLOCK, cols] = e_c.astype(p_ref.dtype)
                    p_ref[j, 2 * BLOCK:BAND, cols] = e_r.astype(p_ref.dtype)
                    p_ref[j, BAND:BAND + N_META, cols] = e_m.astype(p_ref.dtype)

    heads_per_pair = HEADS_PER_LANE_GROUP * GQA_GROUP
    for j in range(n_pairs):
        out_t = _dot(vall_ref[j * LANES:(j + 1) * LANES, :], p_ref[j])
        for g in range(GQA_GROUP):
            c0, c1 = g * BLOCK, (GQA_GROUP + g) * BLOCK
            og_t = jnp.where(upper, out_t[:, c0:c0 + BLOCK] * inv[j * heads_per_pair + g],
                             out_t[:, c1:c1 + BLOCK] * inv[j * heads_per_pair + GQA_GROUP + g])
            lanes = slice((j * GQA_GROUP + g) * LANES, (j * GQA_GROUP + g + 1) * LANES)
            o_ref[pl.ds(row0, BLOCK), lanes] = og_t.T.astype(o_ref.dtype)


def _attn_real_kernel(exact_ref, sink_ref, shift_ref, q_ref, k_ref, vt_ref, km_ref, vmt_ref, o_ref,
                      bias_l_ref, bias_c_ref, bias_r_ref, bias_m_ref, vall_ref, p_ref):
    first = (pl.program_id(0) == 0) & (pl.program_id(1) == 0)
    n_heads = q_ref.shape[1] // HEAD_DIM
    per_step = q_ref.shape[0] // BLOCK
    last = pl.num_programs(1) * per_step - 1

    @pl.when(first)
    def _init_tables():
        key = lax.broadcasted_iota(jnp.int32, (BLOCK, BLOCK), 0)
        qry = lax.broadcasted_iota(jnp.int32, (BLOCK, BLOCK), 1)
        masked = jnp.full((BLOCK, BLOCK), NEG_INF, F32)
        meta = lax.broadcasted_iota(jnp.int32, (N_META, BLOCK), 0)
        mqry = lax.broadcasted_iota(jnp.int32, (N_META, BLOCK), 1)
        for h in range(n_heads):
            slope = _slope2(h, n_heads)
            shift = shift_ref[h]
            for ref, delta in ((bias_l_ref, key - BLOCK - qry), (bias_c_ref, key - qry), (bias_r_ref, key + BLOCK - qry)):
                dist = jnp.abs(delta)
                table = jnp.where(dist <= WINDOW, -slope * dist.astype(F32) - shift, NEG_INF)
                if ref is bias_c_ref:
                    ref[h] = table
                else:
                    ref[0, h] = table
                    ref[1, h] = masked
            bias_m_ref[h] = -slope * (N_META + mqry - meta).astype(F32) - shift
        vall_ref[...] = jnp.zeros_like(vall_ref)
        p_ref[...] = jnp.zeros_like(p_ref)

    refs = (sink_ref, shift_ref, q_ref, k_ref, vt_ref, km_ref, vmt_ref, o_ref,
            bias_l_ref, bias_c_ref, bias_r_ref, bias_m_ref, vall_ref, p_ref)
    needs_max = exact_ref[0] != 0

    def block(sub, carry):
        n = pl.program_id(1) * per_step + sub
        row0 = pl.multiple_of(sub * BLOCK, BLOCK)

        @pl.when(needs_max)
        def _exact():
            _attend_block(n, last, row0, True, *refs)

        @pl.when(jnp.logical_not(needs_max))
        def _bounded():
            _attend_block(n, last, row0, False, *refs)

        return carry

    lax.fori_loop(0, per_step, block, 0)


def _attn_meta_kernel(sink_ref, q_ref, k_ref, vt_ref, km_ref, vm_ref, o_ref):
    rows = q_ref.shape[0]
    n_heads = q_ref.shape[1] // HEAD_DIM
    keys = jnp.concatenate([k_ref[...], km_ref[...]], axis=0)
    vals = jnp.concatenate([vt_ref[0].astype(F32).T.astype(BF16), vm_ref[...]], axis=0)
    nk = BLOCK + N_META
    qpos = lax.broadcasted_iota(jnp.int32, (rows, nk), 0)
    col = lax.broadcasted_iota(jnp.int32, (rows, nk), 1)
    idist = jnp.abs(qpos - jnp.where(col < BLOCK, N_META + col, col - BLOCK))
    valid = idist <= WINDOW
    dist = idist.astype(F32)
    low = lax.broadcasted_iota(jnp.int32, (rows, LANES), 1) < HEAD_DIM
    q = q_ref[...]
    for j in range(keys.shape[1] // LANES):
        scores = _dot_nt(_stack_queries(q, j), keys[:, j * LANES:(j + 1) * LANES])
        probs = []
        for e in range(HEADS_PER_LANE_GROUP):
            for g in range(GQA_GROUP):
                head = _head_index(j, e, g)
                r0 = (e * GQA_GROUP + g) * rows
                s = jnp.where(valid, scores[r0:r0 + rows] - _slope2(head, n_heads) * dist, NEG_INF)
                sink = sink_ref[head] * LOG2E
                m = jnp.maximum(jnp.max(s, axis=-1, keepdims=True), sink)
                ex = jnp.exp2(s - m)
                denom = jnp.sum(ex, axis=-1, keepdims=True) + jnp.exp2(sink - m)
                probs.append((ex * (1.0 / denom)).astype(BF16))
        out = _dot(jnp.concatenate(probs, axis=0), vals[:, j * LANES:(j + 1) * LANES])
        for g in range(GQA_GROUP):
            og = jnp.where(low, out[g * rows:(g + 1) * rows], out[(GQA_GROUP + g) * rows:(GQA_GROUP + g + 1) * rows])
            lanes = slice((j * GQA_GROUP + g) * LANES, (j * GQA_GROUP + g + 1) * LANES)
            o_ref[:, lanes] = og.astype(o_ref.dtype)


def _attention(sink, logit_bound, q, k, vt, qm, km, vm, vmt):
    b, s, dq = q.shape
    dkv = k.shape[2]
    nb = s // BLOCK
    n_heads = dq // HEAD_DIM
    n_pairs = dkv // LANES
    padded_keys = 2 * MXU_DIM
    rows = min(ATTN_BLOCKS_PER_STEP * BLOCK, s)
    exact = logit_bound > SAFE_LOGIT_BOUND
    shift = jnp.where(exact, 0.0, jnp.maximum(logit_bound, sink * LOG2E))
    smem = pl.BlockSpec(memory_space=pltpu.SMEM)
    o_real = pl.pallas_call(
        _attn_real_kernel,
        out_shape=jax.ShapeDtypeStruct((b, s, dq), BF16),
        grid=(b, s // rows),
        in_specs=[smem, smem, smem,
                  pl.BlockSpec((None, rows, dq), lambda i, n: (i, n, 0)),
                  pl.BlockSpec((None, s, dkv), lambda i, n: (i, 0, 0)),
                  pl.BlockSpec((nb, dkv, BLOCK), lambda i, n: (i, 0, 0)),
                  pl.BlockSpec((None, N_META, dkv), lambda i, n: (i, 0, 0)),
                  pl.BlockSpec((None, dkv, N_META), lambda i, n: (i, 0, 0))],
        out_specs=pl.BlockSpec((None, rows, dq), lambda i, n: (i, n, 0)),
        scratch_shapes=[pltpu.VMEM((2, n_heads, BLOCK, BLOCK), F32), pltpu.VMEM((n_heads, BLOCK, BLOCK), F32),
                        pltpu.VMEM((2, n_heads, BLOCK, BLOCK), F32), pltpu.VMEM((n_heads, N_META, BLOCK), F32),
                        pltpu.VMEM((dkv, padded_keys), BF16),
                        pltpu.VMEM((n_pairs, padded_keys, 2 * GQA_GROUP * BLOCK), BF16)],
        compiler_params=_params(2),
        name="attn_real",
    )(exact.astype(jnp.int32)[None], sink, shift.astype(F32), q, k, vt, km, vmt)
    o_meta = pl.pallas_call(
        _attn_meta_kernel,
        out_shape=jax.ShapeDtypeStruct((b, N_META, dq), BF16),
        grid=(b,),
        in_specs=[smem, pl.BlockSpec((None, N_META, dq), lambda i: (i, 0, 0)),
                  pl.BlockSpec((None, BLOCK, dkv), lambda i: (i, 0, 0)),
                  pl.BlockSpec((1, dkv, BLOCK), lambda i: (i * nb, 0, 0)),
                  pl.BlockSpec((None, N_META, dkv), lambda i: (i, 0, 0)),
                  pl.BlockSpec((None, N_META, dkv), lambda i: (i, 0, 0))],
        out_specs=pl.BlockSpec((None, N_META, dq), lambda i: (i, 0, 0)),
        compiler_params=_params(1),
        name="attn_meta",
    )(sink, qm, k, vt, km, vm)
    return o_real, o_meta


def _pool_kernel(h_ref, prev_ref, next_ref, meta_ref, gain_ref, win_ref, wgrp_ref, scale_ref, wout_ref, o_ref,
                 hn_ref, u_ref, *run_refs, tm, total_len):
    i = pl.program_id(1)
    last = pl.num_programs(1) - 1
    gain = gain_ref[...]
    before = jnp.where(i == 0, meta_ref[...], prev_ref[...])
    hn_ref[0:POOL_HALO, :] = _rms_norm(before, gain).astype(BF16)
    hn_ref[POOL_HALO:POOL_HALO + tm, :] = _rms_norm(h_ref[...], gain).astype(BF16)
    after = _rms_norm(next_ref[...], gain)
    hn_ref[POOL_HALO + tm:, :] = jnp.where(i == last, 0.0, after).astype(BF16)
    span = tm + 2 * POOL_HALO
    u_ref[0:span, :] = _dot(hn_ref[...], win_ref[...])
    u_ref[span:, :] = jnp.zeros((u_ref.shape[0] - span, u_ref.shape[1]), F32)

    gdim = u_ref.shape[1] // len(POOL_WINDOWS)
    runs = {1: (u_ref, 0)}
    src_ref, src_col, width = u_ref, 0, 1
    for k, run_ref in enumerate(run_refs, start=1):
        col = k * gdim
        rows = span - F32_SUBLANES * k
        lo = F32_SUBLANES
        run_ref[lo:lo + rows, :] = (src_ref[lo:lo + rows, col - src_col:]
                                    + src_ref[lo + width:lo + width + rows, col - src_col:])
        src_ref, src_col, width = run_ref, col, 2 * width
        runs[width] = (run_ref, col)

    pos = N_META + i * tm + lax.broadcasted_iota(jnp.int32, (tm, 1), 0)
    mixed = []
    for g, window in enumerate(POOL_WINDOWS):
        half = window // 2
        run_ref, col = runs[half]
        cols = slice(g * gdim - col, (g + 1) * gdim - col)
        total = run_ref[POOL_HALO - half:POOL_HALO - half + tm, cols] + run_ref[POOL_HALO:POOL_HALO + tm, cols]
        count = (half + jnp.minimum(half, total_len - pos)).astype(F32)
        pooled = total / count - u_ref[POOL_HALO:POOL_HALO + tm, g * gdim:(g + 1) * gdim]
        mixed.append(_dot(pooled.astype(BF16), wgrp_ref[g]))
    y = (jnp.concatenate(mixed, axis=1) * scale_ref[...]).astype(BF16)
    o_ref[...] = h_ref[...] + _dot(y, wout_ref[...])


def _pool_mixer(h, h_meta, gain, w_in, w_grp, scale, w_out):
    b, s, d = h.shape
    tm = min(POOL_TILE, s)
    per_tile = tm // POOL_HALO
    n_halo_blocks = s // POOL_HALO
    span = tm + 2 * POOL_HALO
    n_groups = len(POOL_WINDOWS)
    gdim = d // n_groups
    assert POOL_WINDOWS == tuple(2 ** (g + 1) for g in range(n_groups)) and POOL_WINDOWS[-1] // 2 <= F32_SUBLANES
    return pl.pallas_call(
        functools.partial(_pool_kernel, tm=tm, total_len=N_META + s),
        out_shape=jax.ShapeDtypeStruct((b, s, d), h.dtype),
        grid=(b, s // tm),
        in_specs=[pl.BlockSpec((None, tm, d), lambda bi, i: (bi, i, 0)),
                  pl.BlockSpec((None, POOL_HALO, d), lambda bi, i: (bi, jnp.maximum(i * per_tile - 1, 0), 0)),
                  pl.BlockSpec((None, POOL_HALO, d),
                               lambda bi, i: (bi, jnp.minimum((i + 1) * per_tile, n_halo_blocks - 1), 0)),
                  pl.BlockSpec((None, N_META, d), lambda bi, i: (bi, 0, 0)),
                  _resident((1, d)), _resident(w_in.shape), _resident(w_grp.shape), _resident((1, d)),
                  _resident(w_out.shape)],
        out_specs=pl.BlockSpec((None, tm, d), lambda bi, i: (bi, i, 0)),
        scratch_shapes=[pltpu.VMEM((span, d), BF16), pltpu.VMEM((span + F32_SUBLANES, d), F32)]
                       + [pltpu.VMEM((span, d - k * gdim), F32) for k in range(1, n_groups)],
        compiler_params=_params(2),
        name="pool_mixer",
    )(h, h, h, h_meta, gain, w_in, w_grp, scale, w_out)


def _permute_heads(w, n_heads, axis):
    n_pairs = n_heads // (GQA_GROUP * HEADS_PER_LANE_GROUP)
    split = w.shape[:axis] + (n_pairs, HEADS_PER_LANE_GROUP, GQA_GROUP, HEAD_DIM) + w.shape[axis + 1:]
    return jnp.swapaxes(w.reshape(split), axis + 1, axis + 2).reshape(w.shape)


def _segment_mean_matrix():
    seg = np.kron(np.eye(MXU_DIM // HEAD_DIM), np.ones((HEAD_DIM, HEAD_DIM))) / HEAD_DIM
    return jnp.asarray(seg, dtype=BF16)


def kernel(x, meta_tokens, ffn_norm, w_gate_up, w_down, mixer_norm, w_qkv, q_norm, k_norm, sink_logit, w_o,
           w_pool_in, w_pool_group, pool_scale, w_pool_out):
    b, s, d = x.shape
    depth = ffn_norm.shape[0]
    n_heads = sink_logit.shape[1]
    dq = n_heads * HEAD_DIM
    dkv = dq // GQA_GROUP
    assert depth == 2 and s % TOKEN_TILE == 0 and d == dq
    seg = _segment_mean_matrix()

    hr = x.reshape(b * s, d)
    hm = jnp.broadcast_to(meta_tokens[None].astype(x.dtype), (b, N_META, d)).reshape(b * N_META, d)

    def ffn_weights(layer, which):
        return (w_gate_up, (layer, which)), (w_down, (layer, which))

    def gain(layer, which):
        return ffn_norm[layer, which][None]

    w00 = w_gate_up[0, 0].astype(BF16), w_down[0, 0].astype(BF16)
    w_qkv_p = jnp.concatenate([_permute_heads(w_qkv[0][:, :dq], n_heads, axis=1), w_qkv[0][:, dq:]],
                              axis=1).astype(BF16)
    w_o_p = _permute_heads(w_o[0], n_heads, axis=0).astype(BF16)
    qkv_args = (mixer_norm[0][None], w_qkv_p, jnp.tile(q_norm[0], n_heads)[None],
                jnp.tile(k_norm[0], dkv // HEAD_DIM)[None], seg)

    hr, q, k, vt, *w01 = _stage(hr, gain(0, 0), *w00, qkv=qkv_args, cast=ffn_weights(0, 1))
    hm, qm, km, vmt = _stage(hm, gain(0, 0), *w00, qkv=qkv_args)
    vm = vmt.transpose(0, 2, 1).reshape(b, N_META, dkv)
    logit_bound = (HEAD_DIM ** 0.5 * LOG2E * ROUNDING_MARGIN) * jnp.max(jnp.abs(q_norm[0])) * jnp.max(jnp.abs(k_norm[0]))
    o_real, o_meta = _attention(sink_logit[0], logit_bound, q.reshape(b, s, dq), k.reshape(b, s, dkv), vt,
                                qm.reshape(b, N_META, dq), km.reshape(b, N_META, dkv), vm, vm.transpose(0, 2, 1))
    hr, *w10 = _stage(hr, gain(0, 1), *w01, att=o_real.reshape(b * s, dq), wo=w_o_p, cast=ffn_weights(1, 0))
    (hm,) = _stage(hm, gain(0, 1), *w01, att=o_meta.reshape(b * N_META, dq), wo=w_o_p)

    n_grp, gdim = w_pool_group.shape[1:3]
    pool_weights = ((w_pool_in, (0,)), (w_pool_group.reshape(1, n_grp * gdim, gdim), (0,)), (w_pool_out, (0,)))
    hr, *casts = _stage(hr, gain(1, 0), *w10, cast=ffn_weights(1, 1) + pool_weights)
    (hm,) = _stage(hm, gain(1, 0), *w10)
    w11, (w_in, w_grp, w_out) = casts[:2], casts[2:]
    hr = _pool_mixer(hr.reshape(b, s, d), hm.reshape(b, N_META, d), mixer_norm[1][None], w_in,
                     w_grp.reshape(n_grp, gdim, gdim), pool_scale[0][None], w_out)
    (hr,) = _stage(hr.reshape(b * s, d), gain(1, 1), *w11)
    return hr.reshape(b, s, d)
```

```python
import functools

import numpy as np
import jax
import jax.numpy as jnp
from jax import lax
from jax.experimental import pallas as pl
from jax.experimental.pallas import tpu as pltpu

F32 = jnp.float32
BF16 = jnp.bfloat16

N_META = 16
HEAD_DIM = 64
GQA_GROUP = 4
WINDOW = 128
BLOCK = 128
POOL_WINDOWS = (2, 4, 8, 16)
RMS_EPS = 1e-6
NEG_INF = -1e30
LOG2E = 1.4426950408889634

LANES = 128
F32_SUBLANES = 8
BF16_SUBLANES = 16
MXU_DIM = 256
HEADS_PER_LANE_GROUP = LANES // HEAD_DIM
BAND = 3 * BLOCK
POOL_HALO = 16

TOKEN_TILE = 1024
QKV_TILE = 1024
POOL_TILE = 1024
ATTN_BLOCKS_PER_STEP = 16
SAFE_LOGIT_BOUND = 40.0
ROUNDING_MARGIN = 1.02
FF_CHUNK = 256
VMEM_LIMIT = 58 * 1024 * 1024


def _params(n_axes, vmem=VMEM_LIMIT):
    return pltpu.CompilerParams(dimension_semantics=("arbitrary",) * n_axes, vmem_limit_bytes=vmem)


def _resident(shape):
    nd = len(shape)
    return pl.BlockSpec(shape, lambda *_: (0,) * nd, pipeline_mode=pl.Buffered(1))


def _rows(tm, d):
    return pl.BlockSpec((tm, d), lambda i: (i, 0))


def _rms_norm(x, gain):
    return x * lax.rsqrt(jnp.mean(x * x, axis=-1, keepdims=True) + RMS_EPS) * gain


def _dot(a, b):
    return jnp.dot(a, b, preferred_element_type=F32)


def _dot_nt(a, b):
    return lax.dot_general(a, b, (((1,), (1,)), ((), ())), preferred_element_type=F32)


def _head_mean_square(t, seg_ref):
    sq = (t * t).astype(BF16)
    seg = seg_ref[...]
    cols = [_dot(sq[:, c * MXU_DIM:(c + 1) * MXU_DIM], seg) for c in range(t.shape[1] // MXU_DIM)]
    return cols[0] if len(cols) == 1 else jnp.concatenate(cols, axis=1)


def _project_qkv(h, gain_ref, w_ref, qgain_ref, kgain_ref, seg_ref, q_ref, k_ref, vt_ref):
    dq, dkv = q_ref.shape[1], k_ref.shape[1]
    hn = _rms_norm(h, gain_ref[...]).astype(BF16)
    qkv = _dot(hn, w_ref[...])
    q = qkv[:, :dq]
    k = qkv[:, dq:dq + dkv]
    v = qkv[:, dq + dkv:]
    q = q * lax.rsqrt(_head_mean_square(q, seg_ref) + RMS_EPS) * qgain_ref[...] * (HEAD_DIM ** -0.5 * LOG2E)
    k = k * lax.rsqrt(_head_mean_square(k, seg_ref) + RMS_EPS) * kgain_ref[...]
    q_ref[...] = q.astype(q_ref.dtype)
    k_ref[...] = k.astype(k_ref.dtype)
    for t in range(vt_ref.shape[0]):
        vt_ref[t] = v[t * BLOCK:(t + 1) * BLOCK, :].T.astype(vt_ref.dtype)


def _stage_kernel(*refs, d_ff, chunk, pre_proj, post_qkv, n_cast):
    refs = list(refs)

    def take(n):
        taken, refs[:] = refs[:n], refs[n:]
        return taken

    (x_ref,) = take(1)
    att_ref, wo_ref = take(2) if pre_proj else (None, None)
    gain_ref, wgu_ref, wd_ref = take(3)
    qkv_in = take(5) if post_qkv else []
    cast_in = take(n_cast)
    (out_ref,) = take(1)
    qkv_out = take(3) if post_qkv else []
    cast_out = take(n_cast)
    (act_ref,) = take(1)

    x = x_ref[...]
    if pre_proj:
        x = x + _dot(att_ref[...], wo_ref[...])
    xn = _rms_norm(x, gain_ref[...]).astype(BF16)
    for c in range(d_ff // chunk):
        gate = _dot(xn, wgu_ref[:, c * chunk:(c + 1) * chunk])
        up = _dot(xn, wgu_ref[:, d_ff + c * chunk:d_ff + (c + 1) * chunk])
        act_ref[:, c * chunk:(c + 1) * chunk] = (jax.nn.silu(gate) * up).astype(BF16)
    h = x + 0.5 * _dot(act_ref[...], wd_ref[...])
    out_ref[...] = h
    if post_qkv:
        _project_qkv(h, *qkv_in, *qkv_out)
    for src, dst in zip(cast_in, cast_out):
        dst[...] = src[...].astype(dst.dtype)


def _cast_specs(rows, cols, prefix, steps):
    block = next(r for r in range(BF16_SUBLANES, rows + 1, BF16_SUBLANES) if rows % r == 0 and r * steps >= rows)
    last = rows // block - 1
    in_spec = pl.BlockSpec((None,) * len(prefix) + (block, cols), lambda i: (*prefix, jnp.minimum(i, last), 0))
    return in_spec, pl.BlockSpec((block, cols), lambda i: (jnp.minimum(i, last), 0))


def _stage(h, gain, wgu, wd, att=None, wo=None, qkv=None, cast=()):
    t, d = h.shape
    d_ff = wd.shape[0]
    tm = min(TOKEN_TILE if qkv is None else QKV_TILE, t)
    steps = t // tm
    operands, in_specs = [h], [_rows(tm, d)]
    if att is not None:
        operands += [att, wo]
        in_specs += [_rows(tm, att.shape[1]), _resident(wo.shape)]
    operands += [gain, wgu, wd]
    in_specs += [_resident(gain.shape), _resident(wgu.shape), _resident(wd.shape)]
    out_shape, out_specs = [jax.ShapeDtypeStruct((t, d), h.dtype)], [_rows(tm, d)]
    if qkv is not None:
        operands += list(qkv)
        in_specs += [_resident(a.shape) for a in qkv]
        dq, dkv = qkv[2].shape[1], qkv[3].shape[1]
        out_shape += [jax.ShapeDtypeStruct((t, dq), BF16), jax.ShapeDtypeStruct((t, dkv), BF16),
                      jax.ShapeDtypeStruct((t // BLOCK, dkv, BLOCK), BF16)]
        out_specs += [_rows(tm, dq), _rows(tm, dkv), pl.BlockSpec((tm // BLOCK, dkv, BLOCK), lambda i: (i, 0, 0))]
    for w, prefix in cast:
        in_spec, out_spec = _cast_specs(w.shape[-2], w.shape[-1], prefix, steps)
        operands.append(w)
        in_specs.append(in_spec)
        out_shape.append(jax.ShapeDtypeStruct(w.shape[-2:], BF16))
        out_specs.append(out_spec)
    return pl.pallas_call(
        functools.partial(_stage_kernel, d_ff=d_ff, chunk=FF_CHUNK, pre_proj=att is not None,
                          post_qkv=qkv is not None, n_cast=len(cast)),
        out_shape=out_shape,
        grid=(steps,),
        in_specs=in_specs,
        out_specs=out_specs,
        scratch_shapes=[pltpu.VMEM((tm, d_ff), BF16)],
        compiler_params=_params(1),
        name="stage",
    )(*operands)


def _slope2(head, n_heads):
    return 2.0 ** (-8.0 * (head + 1) / n_heads) * LOG2E


def _head_index(j, e, g):
    return (HEADS_PER_LANE_GROUP * j + e) * GQA_GROUP + g


def _stack_queries(q, j):
    rows = q.shape[0]
    low = lax.broadcasted_iota(jnp.int32, (rows, LANES), 1) < HEAD_DIM
    zero = jnp.zeros((rows, LANES), q.dtype)
    groups = [q[:, (j * GQA_GROUP + g) * LANES:(j * GQA_GROUP + g + 1) * LANES] for g in range(GQA_GROUP)]
    return jnp.concatenate([jnp.where(low, qg, zero) for qg in groups]
                           + [jnp.where(low, zero, qg) for qg in groups], axis=0)


def _attend_block(n, last, row0, exact, sink_ref, shift_ref, q_ref, k_ref, vt_ref, km_ref, vmt_ref, o_ref,
                  bias_l_ref, bias_c_ref, bias_r_ref, bias_m_ref, vall_ref, p_ref):
    n_heads = q_ref.shape[1] // HEAD_DIM
    n_pairs = k_ref.shape[1] // LANES
    left = jnp.maximum(n - 1, 0)
    right = jnp.minimum(n + 1, last)
    no_left = (n == 0).astype(jnp.int32)
    no_right = (n == last).astype(jnp.int32)
    blocks = (left, n, right)
    keys = jnp.concatenate([k_ref[pl.ds(pl.multiple_of(blk * BLOCK, BLOCK), BLOCK), :] for blk in blocks]
                           + [km_ref[...]], axis=0)
    for i, blk in enumerate(blocks):
        vall_ref[:, i * BLOCK:(i + 1) * BLOCK] = vt_ref[blk]
    vall_ref[:, BAND:BAND + N_META] = vmt_ref[...]
    q = q_ref[pl.ds(row0, BLOCK), :]
    block_shift = (n * BLOCK).astype(F32)
    upper = lax.broadcasted_iota(jnp.int32, (LANES, BLOCK), 0) < HEAD_DIM
    low = lax.broadcasted_iota(jnp.int32, (BLOCK, LANES), 1) < HEAD_DIM
    zero = jnp.zeros((BLOCK, LANES), q.dtype)
    heads_per_dot = MXU_DIM // BLOCK

    inv = []
    for j in range(n_pairs):
        kg = keys[:, j * LANES:(j + 1) * LANES]
        for e in range(HEADS_PER_LANE_GROUP):
            for g0 in range(0, GQA_GROUP, heads_per_dot):
                groups = [q[:, (j * GQA_GROUP + g) * LANES:(j * GQA_GROUP + g + 1) * LANES]
                          for g in range(g0, g0 + heads_per_dot)]
                halves = [jnp.where(low, qg, zero) if e == 0 else jnp.where(low, zero, qg) for qg in groups]
                st = _dot_nt(kg, jnp.concatenate(halves, axis=0))
                for gi in range(heads_per_dot):
                    head = _head_index(j, e, g0 + gi)
                    src = slice(gi * BLOCK, (gi + 1) * BLOCK)
                    cols = slice((e * GQA_GROUP + g0 + gi) * BLOCK, (e * GQA_GROUP + g0 + gi + 1) * BLOCK)
                    s_l = st[0:BLOCK, src] + bias_l_ref[no_left, head]
                    s_c = st[BLOCK:2 * BLOCK, src] + bias_c_ref[head]
                    s_r = st[2 * BLOCK:BAND, src] + bias_r_ref[no_right, head]
                    s_m = st[BAND:BAND + N_META, src] + (bias_m_ref[head] - _slope2(head, n_heads) * block_shift)
                    sink = sink_ref[head] * LOG2E - shift_ref[head]
                    if exact:
                        m = jnp.max(jnp.maximum(jnp.maximum(s_l, s_c), s_r), axis=0, keepdims=True)
                        m = jnp.maximum(jnp.maximum(m, jnp.max(s_m, axis=0, keepdims=True)), sink)
                        s_l, s_c, s_r, s_m = s_l - m, s_c - m, s_r - m, s_m - m
                        sink_term = jnp.exp2(sink - m)
                    else:
                        sink_term = jnp.exp2(jnp.full((1, BLOCK), sink, F32))
                    e_l, e_c, e_r, e_m = jnp.exp2(s_l), jnp.exp2(s_c), jnp.exp2(s_r), jnp.exp2(s_m)
                    denom = (jnp.sum(e_l + e_c + e_r, axis=0, keepdims=True) + jnp.sum(e_m, axis=0, keepdims=True)
                             + sink_term)
                    inv.append(1.0 / denom)
                    p_ref[j, 0:BLOCK, cols] = e_l.astype(p_ref.dtype)
                    p_ref[j, BLOCK:2 * BLOCK, cols] = e_c.astype(p_ref.dtype)
                    p_ref[j, 2 * BLOCK:BAND, cols] = e_r.astype(p_ref.dtype)
                    p_ref[j, BAND:BAND + N_META, cols] = e_m.astype(p_ref.dtype)

    heads_per_pair = HEADS_PER_LANE_GROUP * GQA_GROUP
    for j in range(n_pairs):
        out_t = _dot(vall_ref[j * LANES:(j + 1) * LANES, :], p_ref[j])
        for g in range(GQA_GROUP):
            c0, c1 = g * BLOCK, (GQA_GROUP + g) * BLOCK
            og_t = jnp.where(upper, out_t[:, c0:c0 + BLOCK] * inv[j * heads_per_pair + g],
                             out_t[:, c1:c1 + BLOCK] * inv[j * heads_per_pair + GQA_GROUP + g])
            lanes = slice((j * GQA_GROUP + g) * LANES, (j * GQA_GROUP + g + 1) * LANES)
            o_ref[pl.ds(row0, BLOCK), lanes] = og_t.T.astype(o_ref.dtype)


def _attn_real_kernel(exact_ref, sink_ref, shift_ref, q_ref, k_ref, vt_ref, km_ref, vmt_ref, o_ref,
                      bias_l_ref, bias_c_ref, bias_r_ref, bias_m_ref, vall_ref, p_ref):
    first = (pl.program_id(0) == 0) & (pl.program_id(1) == 0)
    n_heads = q_ref.shape[1] // HEAD_DIM
    per_step = q_ref.shape[0] // BLOCK
    last = pl.num_programs(1) * per_step - 1

    @pl.when(first)
    def _init_tables():
        key = lax.broadcasted_iota(jnp.int32, (BLOCK, BLOCK), 0)
        qry = lax.broadcasted_iota(jnp.int32, (BLOCK, BLOCK), 1)
        masked = jnp.full((BLOCK, BLOCK), NEG_INF, F32)
        meta = lax.broadcasted_iota(jnp.int32, (N_META, BLOCK), 0)
        mqry = lax.broadcasted_iota(jnp.int32, (N_META, BLOCK), 1)
        for h in range(n_heads):
            slope = _slope2(h, n_heads)
            shift = shift_ref[h]
            for ref, delta in ((bias_l_ref, key - BLOCK - qry), (bias_c_ref, key - qry), (bias_r_ref, key + BLOCK - qry)):
                dist = jnp.abs(delta)
                table = jnp.where(dist <= WINDOW, -slope * dist.astype(F32) - shift, NEG_INF)
                if ref is bias_c_ref:
                    ref[h] = table
                else:
                    ref[0, h] = table
                    ref[1, h] = masked
            bias_m_ref[h] = -slope * (N_META + mqry - meta).astype(F32) - shift
        vall_ref[...] = jnp.zeros_like(vall_ref)
        p_ref[...] = jnp.zeros_like(p_ref)

    refs = (sink_ref, shift_ref, q_ref, k_ref, vt_ref, km_ref, vmt_ref, o_ref,
            bias_l_ref, bias_c_ref, bias_r_ref, bias_m_ref, vall_ref, p_ref)
    needs_max = exact_ref[0] != 0

    def block(sub, carry):
        n = pl.program_id(1) * per_step + sub
        row0 = pl.multiple_of(sub * BLOCK, BLOCK)

        @pl.when(needs_max)
        def _exact():
            _attend_block(n, last, row0, True, *refs)

        @pl.when(jnp.logical_not(needs_max))
        def _bounded():
            _attend_block(n, last, row0, False, *refs)

        return carry

    lax.fori_loop(0, per_step, block, 0)


def _attn_meta_kernel(sink_ref, q_ref, k_ref, vt_ref, km_ref, vm_ref, o_ref):
    rows = q_ref.shape[0]
    n_heads = q_ref.shape[1] // HEAD_DIM
    keys = jnp.concatenate([k_ref[...], km_ref[...]], axis=0)
    vals = jnp.concatenate([vt_ref[0].astype(F32).T.astype(BF16), vm_ref[...]], axis=0)
    nk = BLOCK + N_META
    qpos = lax.broadcasted_iota(jnp.int32, (rows, nk), 0)
    col = lax.broadcasted_iota(jnp.int32, (rows, nk), 1)
    idist = jnp.abs(qpos - jnp.where(col < BLOCK, N_META + col, col - BLOCK))
    valid = idist <= WINDOW
    dist = idist.astype(F32)
    low = lax.broadcasted_iota(jnp.int32, (rows, LANES), 1) < HEAD_DIM
    q = q_ref[...]
    for j in range(keys.shape[1] // LANES):
        scores = _dot_nt(_stack_queries(q, j), keys[:, j * LANES:(j + 1) * LANES])
        probs = []
        for e in range(HEADS_PER_LANE_GROUP):
            for g in range(GQA_GROUP):
                head = _head_index(j, e, g)
                r0 = (e * GQA_GROUP + g) * rows
                s = jnp.where(valid, scores[r0:r0 + rows] - _slope2(head, n_heads) * dist, NEG_INF)
                sink = sink_ref[head] * LOG2E
                m = jnp.maximum(jnp.max(s, axis=-1, keepdims=True), sink)
                ex = jnp.exp2(s - m)
                denom = jnp.sum(ex, axis=-1, keepdims=True) + jnp.exp2(sink - m)
                probs.append((ex * (1.0 / denom)).astype(BF16))
        out = _dot(jnp.concatenate(probs, axis=0), vals[:, j * LANES:(j + 1) * LANES])
        for g in range(GQA_GROUP):
            og = jnp.where(low, out[g * rows:(g + 1) * rows], out[(GQA_GROUP + g) * rows:(GQA_GROUP + g + 1) * rows])
            lanes = slice((j * GQA_GROUP + g) * LANES, (j * GQA_GROUP + g + 1) * LANES)
            o_ref[:, lanes] = og.astype(o_ref.dtype)


def _attention(sink, logit_bound, q, k, vt, qm, km, vm, vmt):
    b, s, dq = q.shape
    dkv = k.shape[2]
    nb = s // BLOCK
    n_heads = dq // HEAD_DIM
    n_pairs = dkv // LANES
    padded_keys = 2 * MXU_DIM
    rows = min(ATTN_BLOCKS_PER_STEP * BLOCK, s)
    exact = logit_bound > SAFE_LOGIT_BOUND
    shift = jnp.where(exact, 0.0, jnp.maximum(logit_bound, sink * LOG2E))
    smem = pl.BlockSpec(memory_space=pltpu.SMEM)
    o_real = pl.pallas_call(
        _attn_real_kernel,
        out_shape=jax.ShapeDtypeStruct((b, s, dq), BF16),
        grid=(b, s // rows),
        in_specs=[smem, smem, smem,
                  pl.BlockSpec((None, rows, dq), lambda i, n: (i, n, 0)),
                  pl.BlockSpec((None, s, dkv), lambda i, n: (i, 0, 0)),
                  pl.BlockSpec((nb, dkv, BLOCK), lambda i, n: (i, 0, 0)),
                  pl.BlockSpec((None, N_META, dkv), lambda i, n: (i, 0, 0)),
                  pl.BlockSpec((None, dkv, N_META), lambda i, n: (i, 0, 0))],
        out_specs=pl.BlockSpec((None, rows, dq), lambda i, n: (i, n, 0)),
        scratch_shapes=[pltpu.VMEM((2, n_heads, BLOCK, BLOCK), F32), pltpu.VMEM((n_heads, BLOCK, BLOCK), F32),
                        pltpu.VMEM((2, n_heads, BLOCK, BLOCK), F32), pltpu.VMEM((n_heads, N_META, BLOCK), F32),
                        pltpu.VMEM((dkv, padded_keys), BF16),
                        pltpu.VMEM((n_pairs, padded_keys, 2 * GQA_GROUP * BLOCK), BF16)],
        compiler_params=_params(2),
        name="attn_real",
    )(exact.astype(jnp.int32)[None], sink, shift.astype(F32), q, k, vt, km, vmt)
    o_meta = pl.pallas_call(
        _attn_meta_kernel,
        out_shape=jax.ShapeDtypeStruct((b, N_META, dq), BF16),
        grid=(b,),
        in_specs=[smem, pl.BlockSpec((None, N_META, dq), lambda i: (i, 0, 0)),
                  pl.BlockSpec((None, BLOCK, dkv), lambda i: (i, 0, 0)),
                  pl.BlockSpec((1, dkv, BLOCK), lambda i: (i * nb, 0, 0)),
                  pl.BlockSpec((None, N_META, dkv), lambda i: (i, 0, 0)),
                  pl.BlockSpec((None, N_META, dkv), lambda i: (i, 0, 0))],
        out_specs=pl.BlockSpec((None, N_META, dq), lambda i: (i, 0, 0)),
        compiler_params=_params(1),
        name="attn_meta",
    )(sink, qm, k, vt, km, vm)
    return o_real, o_meta


def _pool_kernel(h_ref, prev_ref, next_ref, meta_ref, gain_ref, win_ref, wgrp_ref, scale_ref, wout_ref, o_ref,
                 hn_ref, u_ref, *run_refs, tm, total_len):
    i = pl.program_id(1)
    last = pl.num_programs(1) - 1
    gain = gain_ref[...]
    before = jnp.where(i == 0, meta_ref[...], prev_ref[...])
    hn_ref[0:POOL_HALO, :] = _rms_norm(before, gain).astype(BF16)
    hn_ref[POOL_HALO:POOL_HALO + tm, :] = _rms_norm(h_ref[...], gain).astype(BF16)
    after = _rms_norm(next_ref[...], gain)
    hn_ref[POOL_HALO + tm:, :] = jnp.where(i == last, 0.0, after).astype(BF16)
    span = tm + 2 * POOL_HALO
    u_ref[0:span, :] = _dot(hn_ref[...], win_ref[...])
    u_ref[span:, :] = jnp.zeros((u_ref.shape[0] - span, u_ref.shape[1]), F32)

    gdim = u_ref.shape[1] // len(POOL_WINDOWS)
    runs = {1: (u_ref, 0)}
    src_ref, src_col, width = u_ref, 0, 1
    for k, run_ref in enumerate(run_refs, start=1):
        col = k * gdim
        rows = span - F32_SUBLANES * k
        lo = F32_SUBLANES
        run_ref[lo:lo + rows, :] = (src_ref[lo:lo + rows, col - src_col:]
                                    + src_ref[lo + width:lo + width + rows, col - src_col:])
        src_ref, src_col, width = run_ref, col, 2 * width
        runs[width] = (run_ref, col)

    body = tm - POOL_HALO
    tail_pos = N_META + i * tm + body + lax.broadcasted_iota(jnp.int32, (POOL_HALO, 1), 0)
    mixed = []
    for g, window in enumerate(POOL_WINDOWS):
        half = window // 2
        run_ref, col = runs[half]
        cols = slice(g * gdim - col, (g + 1) * gdim - col)
        total = run_ref[POOL_HALO - half:POOL_HALO - half + tm, cols] + run_ref[POOL_HALO:POOL_HALO + tm, cols]
        count = (half + jnp.minimum(half, total_len - tail_pos)).astype(F32)
        mean = jnp.concatenate([total[:body] * (1.0 / window), total[body:] / count], axis=0)
        pooled = mean - u_ref[POOL_HALO:POOL_HALO + tm, g * gdim:(g + 1) * gdim]
        mixed.append(_dot(pooled.astype(BF16), wgrp_ref[g]))
    y = (jnp.concatenate(mixed, axis=1) * scale_ref[...]).astype(BF16)
    o_ref[...] = h_ref[...] + _dot(y, wout_ref[...])


def _pool_mixer(h, h_meta, gain, w_in, w_grp, scale, w_out):
    b, s, d = h.shape
    tm = min(POOL_TILE, s)
    per_tile = tm // POOL_HALO
    n_halo_blocks = s // POOL_HALO
    span = tm + 2 * POOL_HALO
    n_groups = len(POOL_WINDOWS)
    gdim = d // n_groups
    assert POOL_WINDOWS == tuple(2 ** (g + 1) for g in range(n_groups)) and POOL_WINDOWS[-1] // 2 <= F32_SUBLANES
    return pl.pallas_call(
        functools.partial(_pool_kernel, tm=tm, total_len=N_META + s),
        out_shape=jax.ShapeDtypeStruct((b, s, d), h.dtype),
        grid=(b, s // tm),
        in_specs=[pl.BlockSpec((None, tm, d), lambda bi, i: (bi, i, 0)),
                  pl.BlockSpec((None, POOL_HALO, d), lambda bi, i: (bi, jnp.maximum(i * per_tile - 1, 0), 0)),
                  pl.BlockSpec((None, POOL_HALO, d),
                               lambda bi, i: (bi, jnp.minimum((i + 1) * per_tile, n_halo_blocks - 1), 0)),
                  pl.BlockSpec((None, N_META, d), lambda bi, i: (bi, 0, 0)),
                  _resident((1, d)), _resident(w_in.shape), _resident(w_grp.shape), _resident((1, d)),
                  _resident(w_out.shape)],
        out_specs=pl.BlockSpec((None, tm, d), lambda bi, i: (bi, i, 0)),
        scratch_shapes=[pltpu.VMEM((span, d), BF16), pltpu.VMEM((span + F32_SUBLANES, d), F32)]
                       + [pltpu.VMEM((span, d - k * gdim), F32) for k in range(1, n_groups)],
        compiler_params=_params(2),
        name="pool_mixer",
    )(h, h, h, h_meta, gain, w_in, w_grp, scale, w_out)


def _permute_heads(w, n_heads, axis):
    n_pairs = n_heads // (GQA_GROUP * HEADS_PER_LANE_GROUP)
    split = w.shape[:axis] + (n_pairs, HEADS_PER_LANE_GROUP, GQA_GROUP, HEAD_DIM) + w.shape[axis + 1:]
    return jnp.swapaxes(w.reshape(split), axis + 1, axis + 2).reshape(w.shape)


def _segment_mean_matrix():
    seg = np.kron(np.eye(MXU_DIM // HEAD_DIM), np.ones((HEAD_DIM, HEAD_DIM))) / HEAD_DIM
    return jnp.asarray(seg, dtype=BF16)


def kernel(x, meta_tokens, ffn_norm, w_gate_up, w_down, mixer_norm, w_qkv, q_norm, k_norm, sink_logit, w_o,
           w_pool_in, w_pool_group, pool_scale, w_pool_out):
    b, s, d = x.shape
    depth = ffn_norm.shape[0]
    n_heads = sink_logit.shape[1]
    dq = n_heads * HEAD_DIM
    dkv = dq // GQA_GROUP
    assert depth == 2 and s % TOKEN_TILE == 0 and d == dq
    seg = _segment_mean_matrix()

    hr = x.reshape(b * s, d)
    hm = jnp.broadcast_to(meta_tokens[None].astype(x.dtype), (b, N_META, d)).reshape(b * N_META, d)

    def ffn_weights(layer, which):
        return (w_gate_up, (layer, which)), (w_down, (layer, which))

    def gain(layer, which):
        return ffn_norm[layer, which][None]

    w00 = w_gate_up[0, 0].astype(BF16), w_down[0, 0].astype(BF16)
    w_qkv_p = jnp.concatenate([_permute_heads(w_qkv[0][:, :dq], n_heads, axis=1), w_qkv[0][:, dq:]],
                              axis=1).astype(BF16)
    w_o_p = _permute_heads(w_o[0], n_heads, axis=0).astype(BF16)
    qkv_args = (mixer_norm[0][None], w_qkv_p, jnp.tile(q_norm[0], n_heads)[None],
                jnp.tile(k_norm[0], dkv // HEAD_DIM)[None], seg)

    hr, q, k, vt, *w01 = _stage(hr, gain(0, 0), *w00, qkv=qkv_args, cast=ffn_weights(0, 1))
    hm, qm, km, vmt = _stage(hm, gain(0, 0), *w00, qkv=qkv_args)
    vm = vmt.transpose(0, 2, 1).reshape(b, N_META, dkv)
    logit_bound = (HEAD_DIM ** 0.5 * LOG2E * ROUNDING_MARGIN) * jnp.max(jnp.abs(q_norm[0])) * jnp.max(jnp.abs(k_norm[0]))
    o_real, o_meta = _attention(sink_logit[0], logit_bound, q.reshape(b, s, dq), k.reshape(b, s, dkv), vt,
                                qm.reshape(b, N_META, dq), km.reshape(b, N_META, dkv), vm, vm.transpose(0, 2, 1))
    hr, *w10 = _stage(hr, gain(0, 1), *w01, att=o_real.reshape(b * s, dq), wo=w_o_p, cast=ffn_weights(1, 0))
    (hm,) = _stage(hm, gain(0, 1), *w01, att=o_meta.reshape(b * N_META, dq), wo=w_o_p)

    n_grp, gdim = w_pool_group.shape[1:3]
    pool_weights = ((w_pool_in, (0,)), (w_pool_group.reshape(1, n_grp * gdim, gdim), (0,)), (w_pool_out, (0,)))
    hr, *casts = _stage(hr, gain(1, 0), *w10, cast=ffn_weights(1, 1) + pool_weights)
    (hm,) = _stage(hm, gain(1, 0), *w10)
    w11, (w_in, w_grp, w_out) = casts[:2], casts[2:]
    hr = _pool_mixer(hr.reshape(b, s, d), hm.reshape(b, N_META, d), mixer_norm[1][None], w_in,
                     w_grp.reshape(n_grp, gdim, gdim), pool_scale[0][None], w_out)
    (hr,) = _stage(hr.reshape(b * s, d), gain(1, 1), *w11)
    return hr.reshape(b, s, d)
```

```python
import functools

import numpy as np
import jax
import jax.numpy as jnp
from jax import lax
from jax.experimental import pallas as pl
from jax.experimental.pallas import tpu as pltpu

F32 = jnp.float32
BF16 = jnp.bfloat16

N_META = 16
HEAD_DIM = 64
GQA_GROUP = 4
WINDOW = 128
BLOCK = 128
POOL_WINDOWS = (2, 4, 8, 16)
RMS_EPS = 1e-6
NEG_INF = -1e30
LOG2E = 1.4426950408889634

LANES = 128
F32_SUBLANES = 8
BF16_SUBLANES = 16
MXU_DIM = 256
HEADS_PER_LANE_GROUP = LANES // HEAD_DIM
BAND = 3 * BLOCK
POOL_HALO = 16

TOKEN_TILE = 1024
QKV_TILE = 1024
POOL_TILE = 1024
ATTN_BLOCKS_PER_STEP = 16
ATTN_BLOCK_GROUP = 8
SAFE_LOGIT_BOUND = 40.0
ROUNDING_MARGIN = 1.02
FF_CHUNK = 256
VMEM_LIMIT = 58 * 1024 * 1024


def _params(n_axes, vmem=VMEM_LIMIT):
    return pltpu.CompilerParams(dimension_semantics=("arbitrary",) * n_axes, vmem_limit_bytes=vmem)


def _resident(shape):
    nd = len(shape)
    return pl.BlockSpec(shape, lambda *_: (0,) * nd, pipeline_mode=pl.Buffered(1))


def _rows(tm, d):
    return pl.BlockSpec((tm, d), lambda i: (i, 0))


def _rms_norm(x, gain):
    return x * lax.rsqrt(jnp.mean(x * x, axis=-1, keepdims=True) + RMS_EPS) * gain


def _dot(a, b):
    return jnp.dot(a, b, preferred_element_type=F32)


def _dot_nt(a, b):
    return lax.dot_general(a, b, (((1,), (1,)), ((), ())), preferred_element_type=F32)


def _head_mean_square(t, seg_ref):
    sq = (t * t).astype(BF16)
    seg = seg_ref[...]
    cols = [_dot(sq[:, c * MXU_DIM:(c + 1) * MXU_DIM], seg) for c in range(t.shape[1] // MXU_DIM)]
    return cols[0] if len(cols) == 1 else jnp.concatenate(cols, axis=1)


def _project_qkv(h, gain_ref, w_ref, qgain_ref, kgain_ref, seg_ref, q_ref, k_ref, vt_ref):
    dq, dkv = q_ref.shape[1], k_ref.shape[1]
    hn = _rms_norm(h, gain_ref[...]).astype(BF16)
    qkv = _dot(hn, w_ref[...])
    q = qkv[:, :dq]
    k = qkv[:, dq:dq + dkv]
    v = qkv[:, dq + dkv:]
    q = q * lax.rsqrt(_head_mean_square(q, seg_ref) + RMS_EPS) * qgain_ref[...] * (HEAD_DIM ** -0.5 * LOG2E)
    k = k * lax.rsqrt(_head_mean_square(k, seg_ref) + RMS_EPS) * kgain_ref[...]
    q_ref[...] = q.astype(q_ref.dtype)
    k_ref[...] = k.astype(k_ref.dtype)
    for t in range(vt_ref.shape[0]):
        vt_ref[t] = v[t * BLOCK:(t + 1) * BLOCK, :].T.astype(vt_ref.dtype)


def _stage_kernel(*refs, d_ff, chunk, pre_proj, post_qkv, n_cast):
    refs = list(refs)

    def take(n):
        taken, refs[:] = refs[:n], refs[n:]
        return taken

    (x_ref,) = take(1)
    att_ref, wo_ref = take(2) if pre_proj else (None, None)
    gain_ref, wgu_ref, wd_ref = take(3)
    qkv_in = take(5) if post_qkv else []
    cast_in = take(n_cast)
    (out_ref,) = take(1)
    qkv_out = take(3) if post_qkv else []
    cast_out = take(n_cast)
    (act_ref,) = take(1)

    x = x_ref[...]
    if pre_proj:
        x = x + _dot(att_ref[...], wo_ref[...])
    xn = _rms_norm(x, gain_ref[...]).astype(BF16)
    for c in range(d_ff // chunk):
        gate = _dot(xn, wgu_ref[:, c * chunk:(c + 1) * chunk])
        up = _dot(xn, wgu_ref[:, d_ff + c * chunk:d_ff + (c + 1) * chunk])
        act_ref[:, c * chunk:(c + 1) * chunk] = (jax.nn.silu(gate) * up).astype(BF16)
    h = x + 0.5 * _dot(act_ref[...], wd_ref[...])
    out_ref[...] = h
    if post_qkv:
        _project_qkv(h, *qkv_in, *qkv_out)
    for src, dst in zip(cast_in, cast_out):
        dst[...] = src[...].astype(dst.dtype)


def _cast_specs(rows, cols, prefix, steps):
    block = next(r for r in range(BF16_SUBLANES, rows + 1, BF16_SUBLANES) if rows % r == 0 and r * steps >= rows)
    last = rows // block - 1
    in_spec = pl.BlockSpec((None,) * len(prefix) + (block, cols), lambda i: (*prefix, jnp.minimum(i, last), 0))
    return in_spec, pl.BlockSpec((block, cols), lambda i: (jnp.minimum(i, last), 0))


def _stage(h, gain, wgu, wd, att=None, wo=None, qkv=None, cast=()):
    t, d = h.shape
    d_ff = wd.shape[0]
    tm = min(TOKEN_TILE if qkv is None else QKV_TILE, t)
    steps = t // tm
    operands, in_specs = [h], [_rows(tm, d)]
    if att is not None:
        operands += [att, wo]
        in_specs += [_rows(tm, att.shape[1]), _resident(wo.shape)]
    operands += [gain, wgu, wd]
    in_specs += [_resident(gain.shape), _resident(wgu.shape), _resident(wd.shape)]
    out_shape, out_specs = [jax.ShapeDtypeStruct((t, d), h.dtype)], [_rows(tm, d)]
    if qkv is not None:
        operands += list(qkv)
        in_specs += [_resident(a.shape) for a in qkv]
        dq, dkv = qkv[2].shape[1], qkv[3].shape[1]
        out_shape += [jax.ShapeDtypeStruct((t, dq), BF16), jax.ShapeDtypeStruct((t, dkv), BF16),
                      jax.ShapeDtypeStruct((t // BLOCK, dkv, BLOCK), BF16)]
        out_specs += [_rows(tm, dq), _rows(tm, dkv), pl.BlockSpec((tm // BLOCK, dkv, BLOCK), lambda i: (i, 0, 0))]
    for w, prefix in cast:
        in_spec, out_spec = _cast_specs(w.shape[-2], w.shape[-1], prefix, steps)
        operands.append(w)
        in_specs.append(in_spec)
        out_shape.append(jax.ShapeDtypeStruct(w.shape[-2:], BF16))
        out_specs.append(out_spec)
    return pl.pallas_call(
        functools.partial(_stage_kernel, d_ff=d_ff, chunk=FF_CHUNK, pre_proj=att is not None,
                          post_qkv=qkv is not None, n_cast=len(cast)),
        out_shape=out_shape,
        grid=(steps,),
        in_specs=in_specs,
        out_specs=out_specs,
        scratch_shapes=[pltpu.VMEM((tm, d_ff), BF16)],
        compiler_params=_params(1),
        name="stage",
    )(*operands)


def _slope2(head, n_heads):
    return 2.0 ** (-8.0 * (head + 1) / n_heads) * LOG2E


def _head_index(j, e, g):
    return (HEADS_PER_LANE_GROUP * j + e) * GQA_GROUP + g


def _stack_queries(q, j):
    rows = q.shape[0]
    low = lax.broadcasted_iota(jnp.int32, (rows, LANES), 1) < HEAD_DIM
    zero = jnp.zeros((rows, LANES), q.dtype)
    groups = [q[:, (j * GQA_GROUP + g) * LANES:(j * GQA_GROUP + g + 1) * LANES] for g in range(GQA_GROUP)]
    return jnp.concatenate([jnp.where(low, qg, zero) for qg in groups]
                           + [jnp.where(low, zero, qg) for qg in groups], axis=0)


def _attend_blocks(n0, last, row0, exact, sink_ref, shift_ref, q_ref, k_ref, vt_ref, km_ref, vmt_ref, o_ref,
                   bias_l_ref, bias_c_ref, bias_r_ref, bias_m_ref, vall_ref, p_ref):
    n_heads = q_ref.shape[1] // HEAD_DIM
    n_pairs = k_ref.shape[1] // LANES
    upper = lax.broadcasted_iota(jnp.int32, (LANES, BLOCK), 0) < HEAD_DIM
    low = lax.broadcasted_iota(jnp.int32, (BLOCK, LANES), 1) < HEAD_DIM
    zero = jnp.zeros((BLOCK, LANES), q_ref.dtype)
    heads_per_dot = MXU_DIM // BLOCK
    heads_per_pair = HEADS_PER_LANE_GROUP * GQA_GROUP

    inv = []
    for slot in range(p_ref.shape[0]):
        n = n0 + slot
        left = jnp.maximum(n - 1, 0)
        right = jnp.minimum(n + 1, last)
        no_left = (n == 0).astype(jnp.int32)
        no_right = (n == last).astype(jnp.int32)
        blocks = (left, n, right)
        keys = jnp.concatenate([k_ref[pl.ds(pl.multiple_of(blk * BLOCK, BLOCK), BLOCK), :] for blk in blocks]
                               + [km_ref[...]], axis=0)
        for i, blk in enumerate(blocks):
            vall_ref[slot, :, i * BLOCK:(i + 1) * BLOCK] = vt_ref[blk]
        vall_ref[slot, :, BAND:BAND + N_META] = vmt_ref[...]
        q = q_ref[pl.ds(row0 + slot * BLOCK, BLOCK), :]
        block_shift = (n * BLOCK).astype(F32)
        for j in range(n_pairs):
            kg = keys[:, j * LANES:(j + 1) * LANES]
            for e in range(HEADS_PER_LANE_GROUP):
                for g0 in range(0, GQA_GROUP, heads_per_dot):
                    groups = [q[:, (j * GQA_GROUP + g) * LANES:(j * GQA_GROUP + g + 1) * LANES]
                              for g in range(g0, g0 + heads_per_dot)]
                    halves = [jnp.where(low, qg, zero) if e == 0 else jnp.where(low, zero, qg) for qg in groups]
                    st = _dot_nt(kg, jnp.concatenate(halves, axis=0))
                    for gi in range(heads_per_dot):
                        head = _head_index(j, e, g0 + gi)
                        src = slice(gi * BLOCK, (gi + 1) * BLOCK)
                        cols = slice((e * GQA_GROUP + g0 + gi) * BLOCK, (e * GQA_GROUP + g0 + gi + 1) * BLOCK)
                        s_l = st[0:BLOCK, src] + bias_l_ref[no_left, head]
                        s_c = st[BLOCK:2 * BLOCK, src] + bias_c_ref[head]
                        s_r = st[2 * BLOCK:BAND, src] + bias_r_ref[no_right, head]
                        s_m = st[BAND:BAND + N_META, src] + (bias_m_ref[head] - _slope2(head, n_heads) * block_shift)
                        sink = sink_ref[head] * LOG2E - shift_ref[head]
                        if exact:
                            m = jnp.max(jnp.maximum(jnp.maximum(s_l, s_c), s_r), axis=0, keepdims=True)
                            m = jnp.maximum(jnp.maximum(m, jnp.max(s_m, axis=0, keepdims=True)), sink)
                            s_l, s_c, s_r, s_m = s_l - m, s_c - m, s_r - m, s_m - m
                            sink_term = jnp.exp2(sink - m)
                        else:
                            sink_term = jnp.exp2(jnp.full((1, BLOCK), sink, F32))
                        e_l, e_c, e_r, e_m = jnp.exp2(s_l), jnp.exp2(s_c), jnp.exp2(s_r), jnp.exp2(s_m)
                        denom = (jnp.sum(e_l + e_c + e_r, axis=0, keepdims=True)
                                 + jnp.sum(e_m, axis=0, keepdims=True) + sink_term)
                        inv.append(1.0 / denom)
                        p_ref[slot, j, 0:BLOCK, cols] = e_l.astype(p_ref.dtype)
                        p_ref[slot, j, BLOCK:2 * BLOCK, cols] = e_c.astype(p_ref.dtype)
                        p_ref[slot, j, 2 * BLOCK:BAND, cols] = e_r.astype(p_ref.dtype)
                        p_ref[slot, j, BAND:BAND + N_META, cols] = e_m.astype(p_ref.dtype)

    for slot in range(p_ref.shape[0]):
        for j in range(n_pairs):
            out_t = _dot(vall_ref[slot, j * LANES:(j + 1) * LANES, :], p_ref[slot, j])
            base = (slot * n_pairs + j) * heads_per_pair
            for g in range(GQA_GROUP):
                c0, c1 = g * BLOCK, (GQA_GROUP + g) * BLOCK
                og_t = jnp.where(upper, out_t[:, c0:c0 + BLOCK] * inv[base + g],
                                 out_t[:, c1:c1 + BLOCK] * inv[base + GQA_GROUP + g])
                lanes = slice((j * GQA_GROUP + g) * LANES, (j * GQA_GROUP + g + 1) * LANES)
                o_ref[pl.ds(row0 + slot * BLOCK, BLOCK), lanes] = og_t.T.astype(o_ref.dtype)


def _attn_real_kernel(exact_ref, sink_ref, shift_ref, q_ref, k_ref, vt_ref, km_ref, vmt_ref, o_ref,
                      bias_l_ref, bias_c_ref, bias_r_ref, bias_m_ref, vall_ref, p_ref):
    first = (pl.program_id(0) == 0) & (pl.program_id(1) == 0)
    n_heads = q_ref.shape[1] // HEAD_DIM
    per_step = q_ref.shape[0] // BLOCK
    last = pl.num_programs(1) * per_step - 1

    @pl.when(first)
    def _init_tables():
        key = lax.broadcasted_iota(jnp.int32, (BLOCK, BLOCK), 0)
        qry = lax.broadcasted_iota(jnp.int32, (BLOCK, BLOCK), 1)
        masked = jnp.full((BLOCK, BLOCK), NEG_INF, F32)
        meta = lax.broadcasted_iota(jnp.int32, (N_META, BLOCK), 0)
        mqry = lax.broadcasted_iota(jnp.int32, (N_META, BLOCK), 1)
        for h in range(n_heads):
            slope = _slope2(h, n_heads)
            shift = shift_ref[h]
            for ref, delta in ((bias_l_ref, key - BLOCK - qry), (bias_c_ref, key - qry), (bias_r_ref, key + BLOCK - qry)):
                dist = jnp.abs(delta)
                table = jnp.where(dist <= WINDOW, -slope * dist.astype(F32) - shift, NEG_INF)
                if ref is bias_c_ref:
                    ref[h] = table
                else:
                    ref[0, h] = table
                    ref[1, h] = masked
            bias_m_ref[h] = -slope * (N_META + mqry - meta).astype(F32) - shift
        vall_ref[...] = jnp.zeros_like(vall_ref)
        p_ref[...] = jnp.zeros_like(p_ref)

    refs = (sink_ref, shift_ref, q_ref, k_ref, vt_ref, km_ref, vmt_ref, o_ref,
            bias_l_ref, bias_c_ref, bias_r_ref, bias_m_ref, vall_ref, p_ref)
    needs_max = exact_ref[0] != 0

    group = p_ref.shape[0]

    def blocks(it, carry):
        n0 = pl.program_id(1) * per_step + it * group
        row0 = pl.multiple_of(it * (group * BLOCK), BLOCK)

        @pl.when(needs_max)
        def _exact():
            _attend_blocks(n0, last, row0, True, *refs)

        @pl.when(jnp.logical_not(needs_max))
        def _bounded():
            _attend_blocks(n0, last, row0, False, *refs)

        return carry

    lax.fori_loop(0, per_step // group, blocks, 0)


def _attn_meta_kernel(sink_ref, q_ref, k_ref, vt_ref, km_ref, vm_ref, o_ref):
    rows = q_ref.shape[0]
    n_heads = q_ref.shape[1] // HEAD_DIM
    keys = jnp.concatenate([k_ref[...], km_ref[...]], axis=0)
    vals = jnp.concatenate([vt_ref[0].astype(F32).T.astype(BF16), vm_ref[...]], axis=0)
    nk = BLOCK + N_META
    qpos = lax.broadcasted_iota(jnp.int32, (rows, nk), 0)
    col = lax.broadcasted_iota(jnp.int32, (rows, nk), 1)
    idist = jnp.abs(qpos - jnp.where(col < BLOCK, N_META + col, col - BLOCK))
    valid = idist <= WINDOW
    dist = idist.astype(F32)
    low = lax.broadcasted_iota(jnp.int32, (rows, LANES), 1) < HEAD_DIM
    q = q_ref[...]
    for j in range(keys.shape[1] // LANES):
        scores = _dot_nt(_stack_queries(q, j), keys[:, j * LANES:(j + 1) * LANES])
        probs = []
        for e in range(HEADS_PER_LANE_GROUP):
            for g in range(GQA_GROUP):
                head = _head_index(j, e, g)
                r0 = (e * GQA_GROUP + g) * rows
                s = jnp.where(valid, scores[r0:r0 + rows] - _slope2(head, n_heads) * dist, NEG_INF)
                sink = sink_ref[head] * LOG2E
                m = jnp.maximum(jnp.max(s, axis=-1, keepdims=True), sink)
                ex = jnp.exp2(s - m)
                denom = jnp.sum(ex, axis=-1, keepdims=True) + jnp.exp2(sink - m)
                probs.append((ex * (1.0 / denom)).astype(BF16))
        out = _dot(jnp.concatenate(probs, axis=0), vals[:, j * LANES:(j + 1) * LANES])
        for g in range(GQA_GROUP):
            og = jnp.where(low, out[g * rows:(g + 1) * rows], out[(GQA_GROUP + g) * rows:(GQA_GROUP + g + 1) * rows])
            lanes = slice((j * GQA_GROUP + g) * LANES, (j * GQA_GROUP + g + 1) * LANES)
            o_ref[:, lanes] = og.astype(o_ref.dtype)


def _attention(sink, logit_bound, q, k, vt, qm, km, vm, vmt):
    b, s, dq = q.shape
    dkv = k.shape[2]
    nb = s // BLOCK
    n_heads = dq // HEAD_DIM
    n_pairs = dkv // LANES
    padded_keys = 2 * MXU_DIM
    rows = min(ATTN_BLOCKS_PER_STEP * BLOCK, s)
    exact = logit_bound > SAFE_LOGIT_BOUND
    shift = jnp.where(exact, 0.0, jnp.maximum(logit_bound, sink * LOG2E))
    smem = pl.BlockSpec(memory_space=pltpu.SMEM)
    o_real = pl.pallas_call(
        _attn_real_kernel,
        out_shape=jax.ShapeDtypeStruct((b, s, dq), BF16),
        grid=(b, s // rows),
        in_specs=[smem, smem, smem,
                  pl.BlockSpec((None, rows, dq), lambda i, n: (i, n, 0)),
                  pl.BlockSpec((None, s, dkv), lambda i, n: (i, 0, 0)),
                  pl.BlockSpec((nb, dkv, BLOCK), lambda i, n: (i, 0, 0)),
                  pl.BlockSpec((None, N_META, dkv), lambda i, n: (i, 0, 0)),
                  pl.BlockSpec((None, dkv, N_META), lambda i, n: (i, 0, 0))],
        out_specs=pl.BlockSpec((None, rows, dq), lambda i, n: (i, n, 0)),
        scratch_shapes=[pltpu.VMEM((2, n_heads, BLOCK, BLOCK), F32), pltpu.VMEM((n_heads, BLOCK, BLOCK), F32),
                        pltpu.VMEM((2, n_heads, BLOCK, BLOCK), F32), pltpu.VMEM((n_heads, N_META, BLOCK), F32),
                        pltpu.VMEM((ATTN_BLOCK_GROUP, dkv, padded_keys), BF16),
                        pltpu.VMEM((ATTN_BLOCK_GROUP, n_pairs, padded_keys, 2 * GQA_GROUP * BLOCK), BF16)],
        compiler_params=_params(2),
        name="attn_real",
    )(exact.astype(jnp.int32)[None], sink, shift.astype(F32), q, k, vt, km, vmt)
    o_meta = pl.pallas_call(
        _attn_meta_kernel,
        out_shape=jax.ShapeDtypeStruct((b, N_META, dq), BF16),
        grid=(b,),
        in_specs=[smem, pl.BlockSpec((None, N_META, dq), lambda i: (i, 0, 0)),
                  pl.BlockSpec((None, BLOCK, dkv), lambda i: (i, 0, 0)),
                  pl.BlockSpec((1, dkv, BLOCK), lambda i: (i * nb, 0, 0)),
                  pl.BlockSpec((None, N_META, dkv), lambda i: (i, 0, 0)),
                  pl.BlockSpec((None, N_META, dkv), lambda i: (i, 0, 0))],
        out_specs=pl.BlockSpec((None, N_META, dq), lambda i: (i, 0, 0)),
        compiler_params=_params(1),
        name="attn_meta",
    )(sink, qm, k, vt, km, vm)
    return o_real, o_meta


def _pool_kernel(h_ref, prev_ref, next_ref, meta_ref, gain_ref, win_ref, wgrp_ref, scale_ref, wout_ref, o_ref,
                 hn_ref, u_ref, *run_refs, tm, total_len):
    i = pl.program_id(1)
    last = pl.num_programs(1) - 1
    gain = gain_ref[...]
    before = jnp.where(i == 0, meta_ref[...], prev_ref[...])
    hn_ref[0:POOL_HALO, :] = _rms_norm(before, gain).astype(BF16)
    hn_ref[POOL_HALO:POOL_HALO + tm, :] = _rms_norm(h_ref[...], gain).astype(BF16)
    after = _rms_norm(next_ref[...], gain)
    hn_ref[POOL_HALO + tm:, :] = jnp.where(i == last, 0.0, after).astype(BF16)
    span = tm + 2 * POOL_HALO
    u_ref[0:span, :] = _dot(hn_ref[...], win_ref[...])
    u_ref[span:, :] = jnp.zeros((u_ref.shape[0] - span, u_ref.shape[1]), F32)

    gdim = u_ref.shape[1] // len(POOL_WINDOWS)
    runs = {1: (u_ref, 0)}
    src_ref, src_col, width = u_ref, 0, 1
    for k, run_ref in enumerate(run_refs, start=1):
        col = k * gdim
        rows = span - F32_SUBLANES * k
        lo = F32_SUBLANES
        run_ref[lo:lo + rows, :] = (src_ref[lo:lo + rows, col - src_col:]
                                    + src_ref[lo + width:lo + width + rows, col - src_col:])
        src_ref, src_col, width = run_ref, col, 2 * width
        runs[width] = (run_ref, col)

    body = tm - POOL_HALO
    tail_pos = N_META + i * tm + body + lax.broadcasted_iota(jnp.int32, (POOL_HALO, 1), 0)
    mixed = []
    for g, window in enumerate(POOL_WINDOWS):
        half = window // 2
        run_ref, col = runs[half]
        cols = slice(g * gdim - col, (g + 1) * gdim - col)
        total = run_ref[POOL_HALO - half:POOL_HALO - half + tm, cols] + run_ref[POOL_HALO:POOL_HALO + tm, cols]
        count = (half + jnp.minimum(half, total_len - tail_pos)).astype(F32)
        mean = jnp.concatenate([total[:body] * (1.0 / window), total[body:] / count], axis=0)
        pooled = mean - u_ref[POOL_HALO:POOL_HALO + tm, g * gdim:(g + 1) * gdim]
        mixed.append(_dot(pooled.astype(BF16), wgrp_ref[g]))
    y = (jnp.concatenate(mixed, axis=1) * scale_ref[...]).astype(BF16)
    o_ref[...] = h_ref[...] + _dot(y, wout_ref[...])


def _pool_mixer(h, h_meta, gain, w_in, w_grp, scale, w_out):
    b, s, d = h.shape
    tm = min(POOL_TILE, s)
    per_tile = tm // POOL_HALO
    n_halo_blocks = s // POOL_HALO
    span = tm + 2 * POOL_HALO
    n_groups = len(POOL_WINDOWS)
    gdim = d // n_groups
    assert POOL_WINDOWS == tuple(2 ** (g + 1) for g in range(n_groups)) and POOL_WINDOWS[-1] // 2 <= F32_SUBLANES
    return pl.pallas_call(
        functools.partial(_pool_kernel, tm=tm, total_len=N_META + s),
        out_shape=jax.ShapeDtypeStruct((b, s, d), h.dtype),
        grid=(b, s // tm),
        in_specs=[pl.BlockSpec((None, tm, d), lambda bi, i: (bi, i, 0)),
                  pl.BlockSpec((None, POOL_HALO, d), lambda bi, i: (bi, jnp.maximum(i * per_tile - 1, 0), 0)),
                  pl.BlockSpec((None, POOL_HALO, d),
                               lambda bi, i: (bi, jnp.minimum((i + 1) * per_tile, n_halo_blocks - 1), 0)),
                  pl.BlockSpec((None, N_META, d), lambda bi, i: (bi, 0, 0)),
                  _resident((1, d)), _resident(w_in.shape), _resident(w_grp.shape), _resident((1, d)),
                  _resident(w_out.shape)],
        out_specs=pl.BlockSpec((None, tm, d), lambda bi, i: (bi, i, 0)),
        scratch_shapes=[pltpu.VMEM((span, d), BF16), pltpu.VMEM((span + F32_SUBLANES, d), F32)]
                       + [pltpu.VMEM((span, d - k * gdim), F32) for k in range(1, n_groups)],
        compiler_params=_params(2),
        name="pool_mixer",
    )(h, h, h, h_meta, gain, w_in, w_grp, scale, w_out)


def _permute_heads(w, n_heads, axis):
    n_pairs = n_heads // (GQA_GROUP * HEADS_PER_LANE_GROUP)
    split = w.shape[:axis] + (n_pairs, HEADS_PER_LANE_GROUP, GQA_GROUP, HEAD_DIM) + w.shape[axis + 1:]
    return jnp.swapaxes(w.reshape(split), axis + 1, axis + 2).reshape(w.shape)


def _segment_mean_matrix():
    seg = np.kron(np.eye(MXU_DIM // HEAD_DIM), np.ones((HEAD_DIM, HEAD_DIM))) / HEAD_DIM
    return jnp.asarray(seg, dtype=BF16)


def kernel(x, meta_tokens, ffn_norm, w_gate_up, w_down, mixer_norm, w_qkv, q_norm, k_norm, sink_logit, w_o,
           w_pool_in, w_pool_group, pool_scale, w_pool_out):
    b, s, d = x.shape
    depth = ffn_norm.shape[0]
    n_heads = sink_logit.shape[1]
    dq = n_heads * HEAD_DIM
    dkv = dq // GQA_GROUP
    assert depth == 2 and s % TOKEN_TILE == 0 and d == dq
    seg = _segment_mean_matrix()

    hr = x.reshape(b * s, d)
    hm = jnp.broadcast_to(meta_tokens[None].astype(x.dtype), (b, N_META, d)).reshape(b * N_META, d)

    def ffn_weights(layer, which):
        return (w_gate_up, (layer, which)), (w_down, (layer, which))

    def gain(layer, which):
        return ffn_norm[layer, which][None]

    w00 = w_gate_up[0, 0].astype(BF16), w_down[0, 0].astype(BF16)
    w_qkv_p = jnp.concatenate([_permute_heads(w_qkv[0][:, :dq], n_heads, axis=1), w_qkv[0][:, dq:]],
                              axis=1).astype(BF16)
    w_o_p = _permute_heads(w_o[0], n_heads, axis=0).astype(BF16)
    qkv_args = (mixer_norm[0][None], w_qkv_p, jnp.tile(q_norm[0], n_heads)[None],
                jnp.tile(k_norm[0], dkv // HEAD_DIM)[None], seg)

    hr, q, k, vt, *w01 = _stage(hr, gain(0, 0), *w00, qkv=qkv_args, cast=ffn_weights(0, 1))
    hm, qm, km, vmt = _stage(hm, gain(0, 0), *w00, qkv=qkv_args)
    vm = vmt.transpose(0, 2, 1).reshape(b, N_META, dkv)
    logit_bound = (HEAD_DIM ** 0.5 * LOG2E * ROUNDING_MARGIN) * jnp.max(jnp.abs(q_norm[0])) * jnp.max(jnp.abs(k_norm[0]))
    o_real, o_meta = _attention(sink_logit[0], logit_bound, q.reshape(b, s, dq), k.reshape(b, s, dkv), vt,
                                qm.reshape(b, N_META, dq), km.reshape(b, N_META, dkv), vm, vm.transpose(0, 2, 1))
    hr, *w10 = _stage(hr, gain(0, 1), *w01, att=o_real.reshape(b * s, dq), wo=w_o_p, cast=ffn_weights(1, 0))
    (hm,) = _stage(hm, gain(0, 1), *w01, att=o_meta.reshape(b * N_META, dq), wo=w_o_p)

    n_grp, gdim = w_pool_group.shape[1:3]
    pool_weights = ((w_pool_in, (0,)), (w_pool_group.reshape(1, n_grp * gdim, gdim), (0,)), (w_pool_out, (0,)))
    hr, *casts = _stage(hr, gain(1, 0), *w10, cast=ffn_weights(1, 1) + pool_weights)
    (hm,) = _stage(hm, gain(1, 0), *w10)
    w11, (w_in, w_grp, w_out) = casts[:2], casts[2:]
    hr = _pool_mixer(hr.reshape(b, s, d), hm.reshape(b, N_META, d), mixer_norm[1][None], w_in,
                     w_grp.reshape(n_grp, gdim, gdim), pool_scale[0][None], w_out)
    (hr,) = _stage(hr.reshape(b * s, d), gain(1, 1), *w11)
    return hr.reshape(b, s, d)
```

```python
import functools

import numpy as np
import jax
import jax.numpy as jnp
from jax import lax
from jax.experimental import pallas as pl
from jax.experimental.pallas import tpu as pltpu

F32 = jnp.float32
BF16 = jnp.bfloat16

N_META = 16
HEAD_DIM = 64
GQA_GROUP = 4
WINDOW = 128
BLOCK = 128
POOL_WINDOWS = (2, 4, 8, 16)
RMS_EPS = 1e-6
NEG_INF = -1e30
LOG2E = 1.4426950408889634

LANES = 128
F32_SUBLANES = 8
BF16_SUBLANES = 16
MXU_DIM = 256
HEADS_PER_LANE_GROUP = LANES // HEAD_DIM
BAND = 3 * BLOCK
POOL_HALO = 16

TOKEN_TILE = 1024
QKV_TILE = 1024
QKV_PIECE = 512
POOL_TILE = 1024
POOL_SUBTILE = 512
ATTN_BLOCKS_PER_STEP = 16
ATTN_BLOCK_GROUP = 8
SAFE_LOGIT_BOUND = 40.0
ROUNDING_MARGIN = 1.02
FF_CHUNK = 256
VMEM_LIMIT = 58 * 1024 * 1024


def _params(n_axes, vmem=VMEM_LIMIT):
    return pltpu.CompilerParams(dimension_semantics=("arbitrary",) * n_axes, vmem_limit_bytes=vmem)


def _resident(shape):
    nd = len(shape)
    return pl.BlockSpec(shape, lambda *_: (0,) * nd, pipeline_mode=pl.Buffered(1))


def _rows(tm, d):
    return pl.BlockSpec((tm, d), lambda i: (i, 0))


def _rms_norm(x, gain):
    return x * lax.rsqrt(jnp.mean(x * x, axis=-1, keepdims=True) + RMS_EPS) * gain


def _dot(a, b):
    return jnp.dot(a, b, preferred_element_type=F32)


def _dot_nt(a, b):
    return lax.dot_general(a, b, (((1,), (1,)), ((), ())), preferred_element_type=F32)


def _head_mean_square(t, seg_ref):
    sq = (t * t).astype(BF16)
    seg = seg_ref[...]
    cols = [_dot(sq[:, c * MXU_DIM:(c + 1) * MXU_DIM], seg) for c in range(t.shape[1] // MXU_DIM)]
    return cols[0] if len(cols) == 1 else jnp.concatenate(cols, axis=1)


def _project_qkv(h, gain_ref, w_ref, qgain_ref, kgain_ref, seg_ref, q_ref, k_ref, vt_ref):
    dq, dkv = q_ref.shape[1], k_ref.shape[1]
    piece = min(h.shape[0], QKV_PIECE)
    starts = range(0, h.shape[0], piece)
    qkvs = [_dot(_rms_norm(h[r:r + piece], gain_ref[...]).astype(BF16), w_ref[...]) for r in starts]
    for r, qkv in zip(starts, qkvs):
        q = qkv[:, :dq]
        k = qkv[:, dq:dq + dkv]
        v = qkv[:, dq + dkv:]
        q = q * lax.rsqrt(_head_mean_square(q, seg_ref) + RMS_EPS) * qgain_ref[...] * (HEAD_DIM ** -0.5 * LOG2E)
        k = k * lax.rsqrt(_head_mean_square(k, seg_ref) + RMS_EPS) * kgain_ref[...]
        q_ref[r:r + piece, :] = q.astype(q_ref.dtype)
        k_ref[r:r + piece, :] = k.astype(k_ref.dtype)
        for t in range(piece // BLOCK):
            vt_ref[r // BLOCK + t] = v[t * BLOCK:(t + 1) * BLOCK, :].T.astype(vt_ref.dtype)


def _stage_kernel(*refs, d_ff, chunk, pre_proj, post_qkv, n_cast):
    refs = list(refs)

    def take(n):
        taken, refs[:] = refs[:n], refs[n:]
        return taken

    (x_ref,) = take(1)
    att_ref, wo_ref = take(2) if pre_proj else (None, None)
    gain_ref, wgu_ref, wd_ref = take(3)
    qkv_in = take(5) if post_qkv else []
    cast_in = take(n_cast)
    (out_ref,) = take(1)
    qkv_out = take(3) if post_qkv else []
    cast_out = take(n_cast)
    (act_ref,) = take(1)

    x = x_ref[...]
    if pre_proj:
        x = x + _dot(att_ref[...], wo_ref[...])
    xn = _rms_norm(x, gain_ref[...]).astype(BF16)
    for c in range(d_ff // chunk):
        gate = _dot(xn, wgu_ref[:, c * chunk:(c + 1) * chunk])
        up = _dot(xn, wgu_ref[:, d_ff + c * chunk:d_ff + (c + 1) * chunk])
        act_ref[:, c * chunk:(c + 1) * chunk] = (jax.nn.silu(gate) * up).astype(BF16)
    h = x + 0.5 * _dot(act_ref[...], wd_ref[...])
    out_ref[...] = h
    if post_qkv:
        _project_qkv(h, *qkv_in, *qkv_out)
    for src, dst in zip(cast_in, cast_out):
        dst[...] = src[...].astype(dst.dtype)


def _cast_specs(rows, cols, prefix, steps):
    block = next(r for r in range(BF16_SUBLANES, rows + 1, BF16_SUBLANES) if rows % r == 0 and r * steps >= rows)
    last = rows // block - 1
    in_spec = pl.BlockSpec((None,) * len(prefix) + (block, cols), lambda i: (*prefix, jnp.minimum(i, last), 0))
    return in_spec, pl.BlockSpec((block, cols), lambda i: (jnp.minimum(i, last), 0))


def _stage(h, gain, wgu, wd, att=None, wo=None, qkv=None, cast=()):
    t, d = h.shape
    d_ff = wd.shape[0]
    tm = min(TOKEN_TILE if qkv is None else QKV_TILE, t)
    steps = t // tm
    operands, in_specs = [h], [_rows(tm, d)]
    if att is not None:
        operands += [att, wo]
        in_specs += [_rows(tm, att.shape[1]), _resident(wo.shape)]
    operands += [gain, wgu, wd]
    in_specs += [_resident(gain.shape), _resident(wgu.shape), _resident(wd.shape)]
    out_shape, out_specs = [jax.ShapeDtypeStruct((t, d), h.dtype)], [_rows(tm, d)]
    if qkv is not None:
        operands += list(qkv)
        in_specs += [_resident(a.shape) for a in qkv]
        dq, dkv = qkv[2].shape[1], qkv[3].shape[1]
        out_shape += [jax.ShapeDtypeStruct((t, dq), BF16), jax.ShapeDtypeStruct((t, dkv), BF16),
                      jax.ShapeDtypeStruct((t // BLOCK, dkv, BLOCK), BF16)]
        out_specs += [_rows(tm, dq), _rows(tm, dkv), pl.BlockSpec((tm // BLOCK, dkv, BLOCK), lambda i: (i, 0, 0))]
    for w, prefix in cast:
        in_spec, out_spec = _cast_specs(w.shape[-2], w.shape[-1], prefix, steps)
        operands.append(w)
        in_specs.append(in_spec)
        out_shape.append(jax.ShapeDtypeStruct(w.shape[-2:], BF16))
        out_specs.append(out_spec)
    return pl.pallas_call(
        functools.partial(_stage_kernel, d_ff=d_ff, chunk=FF_CHUNK, pre_proj=att is not None,
                          post_qkv=qkv is not None, n_cast=len(cast)),
        out_shape=out_shape,
        grid=(steps,),
        in_specs=in_specs,
        out_specs=out_specs,
        scratch_shapes=[pltpu.VMEM((tm, d_ff), BF16)],
        compiler_params=_params(1),
        name="stage",
    )(*operands)


def _slope2(head, n_heads):
    return 2.0 ** (-8.0 * (head + 1) / n_heads) * LOG2E


def _head_index(j, e, g):
    return (HEADS_PER_LANE_GROUP * j + e) * GQA_GROUP + g


def _stack_queries(q, j):
    rows = q.shape[0]
    low = lax.broadcasted_iota(jnp.int32, (rows, LANES), 1) < HEAD_DIM
    zero = jnp.zeros((rows, LANES), q.dtype)
    groups = [q[:, (j * GQA_GROUP + g) * LANES:(j * GQA_GROUP + g + 1) * LANES] for g in range(GQA_GROUP)]
    return jnp.concatenate([jnp.where(low, qg, zero) for qg in groups]
                           + [jnp.where(low, zero, qg) for qg in groups], axis=0)


def _attend_blocks(n0, last, row0, exact, sink_ref, shift_ref, q_ref, k_ref, vt_ref, km_ref, vmt_ref, o_ref,
                   bias_l_ref, bias_c_ref, bias_r_ref, bias_m_ref, vall_ref, p_ref):
    n_heads = q_ref.shape[1] // HEAD_DIM
    n_pairs = k_ref.shape[1] // LANES
    upper = lax.broadcasted_iota(jnp.int32, (LANES, BLOCK), 0) < HEAD_DIM
    low = lax.broadcasted_iota(jnp.int32, (BLOCK, LANES), 1) < HEAD_DIM
    zero = jnp.zeros((BLOCK, LANES), q_ref.dtype)
    heads_per_dot = MXU_DIM // BLOCK
    heads_per_pair = HEADS_PER_LANE_GROUP * GQA_GROUP

    inv = []
    for slot in range(p_ref.shape[0]):
        n = n0 + slot
        left = jnp.maximum(n - 1, 0)
        right = jnp.minimum(n + 1, last)
        no_left = (n == 0).astype(jnp.int32)
        no_right = (n == last).astype(jnp.int32)
        blocks = (left, n, right)
        keys = jnp.concatenate([k_ref[pl.ds(pl.multiple_of(blk * BLOCK, BLOCK), BLOCK), :] for blk in blocks]
                               + [km_ref[...]], axis=0)
        for i, blk in enumerate(blocks):
            vall_ref[slot, :, i * BLOCK:(i + 1) * BLOCK] = vt_ref[blk]
        vall_ref[slot, :, BAND:BAND + N_META] = vmt_ref[...]
        q = q_ref[pl.ds(row0 + slot * BLOCK, BLOCK), :]
        block_shift = (n * BLOCK).astype(F32)
        for j in range(n_pairs):
            kg = keys[:, j * LANES:(j + 1) * LANES]
            for e in range(HEADS_PER_LANE_GROUP):
                for g0 in range(0, GQA_GROUP, heads_per_dot):
                    groups = [q[:, (j * GQA_GROUP + g) * LANES:(j * GQA_GROUP + g + 1) * LANES]
                              for g in range(g0, g0 + heads_per_dot)]
                    halves = [jnp.where(low, qg, zero) if e == 0 else jnp.where(low, zero, qg) for qg in groups]
                    st = _dot_nt(kg, jnp.concatenate(halves, axis=0))
                    for gi in range(heads_per_dot):
                        head = _head_index(j, e, g0 + gi)
                        src = slice(gi * BLOCK, (gi + 1) * BLOCK)
                        cols = slice((e * GQA_GROUP + g0 + gi) * BLOCK, (e * GQA_GROUP + g0 + gi + 1) * BLOCK)
                        s_l = st[0:BLOCK, src] + bias_l_ref[no_left, head]
                        s_c = st[BLOCK:2 * BLOCK, src] + bias_c_ref[head]
                        s_r = st[2 * BLOCK:BAND, src] + bias_r_ref[no_right, head]
                        s_m = st[BAND:BAND + N_META, src] + (bias_m_ref[head] - _slope2(head, n_heads) * block_shift)
                        sink = sink_ref[head] * LOG2E - shift_ref[head]
                        if exact:
                            m = jnp.max(jnp.maximum(jnp.maximum(s_l, s_c), s_r), axis=0, keepdims=True)
                            m = jnp.maximum(jnp.maximum(m, jnp.max(s_m, axis=0, keepdims=True)), sink)
                            s_l, s_c, s_r, s_m = s_l - m, s_c - m, s_r - m, s_m - m
                            sink_term = jnp.exp2(sink - m)
                        else:
                            sink_term = jnp.exp2(jnp.full((1, BLOCK), sink, F32))
                        e_l, e_c, e_r, e_m = jnp.exp2(s_l), jnp.exp2(s_c), jnp.exp2(s_r), jnp.exp2(s_m)
                        denom = (jnp.sum(e_l + e_c + e_r, axis=0, keepdims=True)
                                 + jnp.sum(e_m, axis=0, keepdims=True) + sink_term)
                        inv.append(1.0 / denom)
                        p_ref[slot, j, 0:BLOCK, cols] = e_l.astype(p_ref.dtype)
                        p_ref[slot, j, BLOCK:2 * BLOCK, cols] = e_c.astype(p_ref.dtype)
                        p_ref[slot, j, 2 * BLOCK:BAND, cols] = e_r.astype(p_ref.dtype)
                        p_ref[slot, j, BAND:BAND + N_META, cols] = e_m.astype(p_ref.dtype)

    for slot in range(p_ref.shape[0]):
        for j in range(n_pairs):
            out_t = _dot(vall_ref[slot, j * LANES:(j + 1) * LANES, :], p_ref[slot, j])
            base = (slot * n_pairs + j) * heads_per_pair
            for g in range(GQA_GROUP):
                c0, c1 = g * BLOCK, (GQA_GROUP + g) * BLOCK
                og_t = jnp.where(upper, out_t[:, c0:c0 + BLOCK] * inv[base + g],
                                 out_t[:, c1:c1 + BLOCK] * inv[base + GQA_GROUP + g])
                lanes = slice((j * GQA_GROUP + g) * LANES, (j * GQA_GROUP + g + 1) * LANES)
                o_ref[pl.ds(row0 + slot * BLOCK, BLOCK), lanes] = og_t.T.astype(o_ref.dtype)


def _attn_real_kernel(exact_ref, sink_ref, shift_ref, q_ref, k_ref, vt_ref, km_ref, vmt_ref, o_ref,
                      bias_l_ref, bias_c_ref, bias_r_ref, bias_m_ref, vall_ref, p_ref):
    first = (pl.program_id(0) == 0) & (pl.program_id(1) == 0)
    n_heads = q_ref.shape[1] // HEAD_DIM
    per_step = q_ref.shape[0] // BLOCK
    last = pl.num_programs(1) * per_step - 1

    @pl.when(first)
    def _init_tables():
        key = lax.broadcasted_iota(jnp.int32, (BLOCK, BLOCK), 0)
        qry = lax.broadcasted_iota(jnp.int32, (BLOCK, BLOCK), 1)
        masked = jnp.full((BLOCK, BLOCK), NEG_INF, F32)
        meta = lax.broadcasted_iota(jnp.int32, (N_META, BLOCK), 0)
        mqry = lax.broadcasted_iota(jnp.int32, (N_META, BLOCK), 1)
        for h in range(n_heads):
            slope = _slope2(h, n_heads)
            shift = shift_ref[h]
            for ref, delta in ((bias_l_ref, key - BLOCK - qry), (bias_c_ref, key - qry), (bias_r_ref, key + BLOCK - qry)):
                dist = jnp.abs(delta)
                table = jnp.where(dist <= WINDOW, -slope * dist.astype(F32) - shift, NEG_INF)
                if ref is bias_c_ref:
                    ref[h] = table
                else:
                    ref[0, h] = table
                    ref[1, h] = masked
            bias_m_ref[h] = -slope * (N_META + mqry - meta).astype(F32) - shift
        vall_ref[...] = jnp.zeros_like(vall_ref)
        p_ref[...] = jnp.zeros_like(p_ref)

    refs = (sink_ref, shift_ref, q_ref, k_ref, vt_ref, km_ref, vmt_ref, o_ref,
            bias_l_ref, bias_c_ref, bias_r_ref, bias_m_ref, vall_ref, p_ref)
    needs_max = exact_ref[0] != 0

    group = p_ref.shape[0]

    def blocks(it, carry):
        n0 = pl.program_id(1) * per_step + it * group
        row0 = pl.multiple_of(it * (group * BLOCK), BLOCK)

        @pl.when(needs_max)
        def _exact():
            _attend_blocks(n0, last, row0, True, *refs)

        @pl.when(jnp.logical_not(needs_max))
        def _bounded():
            _attend_blocks(n0, last, row0, False, *refs)

        return carry

    lax.fori_loop(0, per_step // group, blocks, 0)


def _attn_meta_kernel(sink_ref, q_ref, k_ref, vt_ref, km_ref, vm_ref, o_ref):
    rows = q_ref.shape[0]
    n_heads = q_ref.shape[1] // HEAD_DIM
    keys = jnp.concatenate([k_ref[...], km_ref[...]], axis=0)
    vals = jnp.concatenate([vt_ref[0].astype(F32).T.astype(BF16), vm_ref[...]], axis=0)
    nk = BLOCK + N_META
    qpos = lax.broadcasted_iota(jnp.int32, (rows, nk), 0)
    col = lax.broadcasted_iota(jnp.int32, (rows, nk), 1)
    idist = jnp.abs(qpos - jnp.where(col < BLOCK, N_META + col, col - BLOCK))
    valid = idist <= WINDOW
    dist = idist.astype(F32)
    low = lax.broadcasted_iota(jnp.int32, (rows, LANES), 1) < HEAD_DIM
    q = q_ref[...]
    for j in range(keys.shape[1] // LANES):
        scores = _dot_nt(_stack_queries(q, j), keys[:, j * LANES:(j + 1) * LANES])
        probs = []
        for e in range(HEADS_PER_LANE_GROUP):
            for g in range(GQA_GROUP):
                head = _head_index(j, e, g)
                r0 = (e * GQA_GROUP + g) * rows
                s = jnp.where(valid, scores[r0:r0 + rows] - _slope2(head, n_heads) * dist, NEG_INF)
                sink = sink_ref[head] * LOG2E
                m = jnp.maximum(jnp.max(s, axis=-1, keepdims=True), sink)
                ex = jnp.exp2(s - m)
                denom = jnp.sum(ex, axis=-1, keepdims=True) + jnp.exp2(sink - m)
                probs.append((ex * (1.0 / denom)).astype(BF16))
        out = _dot(jnp.concatenate(probs, axis=0), vals[:, j * LANES:(j + 1) * LANES])
        for g in range(GQA_GROUP):
            og = jnp.where(low, out[g * rows:(g + 1) * rows], out[(GQA_GROUP + g) * rows:(GQA_GROUP + g + 1) * rows])
            lanes = slice((j * GQA_GROUP + g) * LANES, (j * GQA_GROUP + g + 1) * LANES)
            o_ref[:, lanes] = og.astype(o_ref.dtype)


def _attention(sink, logit_bound, q, k, vt, qm, km, vm, vmt):
    b, s, dq = q.shape
    dkv = k.shape[2]
    nb = s // BLOCK
    n_heads = dq // HEAD_DIM
    n_pairs = dkv // LANES
    padded_keys = 2 * MXU_DIM
    rows = min(ATTN_BLOCKS_PER_STEP * BLOCK, s)
    exact = logit_bound > SAFE_LOGIT_BOUND
    shift = jnp.where(exact, 0.0, jnp.maximum(logit_bound, sink * LOG2E))
    smem = pl.BlockSpec(memory_space=pltpu.SMEM)
    o_real = pl.pallas_call(
        _attn_real_kernel,
        out_shape=jax.ShapeDtypeStruct((b, s, dq), BF16),
        grid=(b, s // rows),
        in_specs=[smem, smem, smem,
                  pl.BlockSpec((None, rows, dq), lambda i, n: (i, n, 0)),
                  pl.BlockSpec((None, s, dkv), lambda i, n: (i, 0, 0)),
                  pl.BlockSpec((nb, dkv, BLOCK), lambda i, n: (i, 0, 0)),
                  pl.BlockSpec((None, N_META, dkv), lambda i, n: (i, 0, 0)),
                  pl.BlockSpec((None, dkv, N_META), lambda i, n: (i, 0, 0))],
        out_specs=pl.BlockSpec((None, rows, dq), lambda i, n: (i, n, 0)),
        scratch_shapes=[pltpu.VMEM((2, n_heads, BLOCK, BLOCK), F32), pltpu.VMEM((n_heads, BLOCK, BLOCK), F32),
                        pltpu.VMEM((2, n_heads, BLOCK, BLOCK), F32), pltpu.VMEM((n_heads, N_META, BLOCK), F32),
                        pltpu.VMEM((ATTN_BLOCK_GROUP, dkv, padded_keys), BF16),
                        pltpu.VMEM((ATTN_BLOCK_GROUP, n_pairs, padded_keys, 2 * GQA_GROUP * BLOCK), BF16)],
        compiler_params=_params(2),
        name="attn_real",
    )(exact.astype(jnp.int32)[None], sink, shift.astype(F32), q, k, vt, km, vmt)
    o_meta = pl.pallas_call(
        _attn_meta_kernel,
        out_shape=jax.ShapeDtypeStruct((b, N_META, dq), BF16),
        grid=(b,),
        in_specs=[smem, pl.BlockSpec((None, N_META, dq), lambda i: (i, 0, 0)),
                  pl.BlockSpec((None, BLOCK, dkv), lambda i: (i, 0, 0)),
                  pl.BlockSpec((1, dkv, BLOCK), lambda i: (i * nb, 0, 0)),
                  pl.BlockSpec((None, N_META, dkv), lambda i: (i, 0, 0)),
                  pl.BlockSpec((None, N_META, dkv), lambda i: (i, 0, 0))],
        out_specs=pl.BlockSpec((None, N_META, dq), lambda i: (i, 0, 0)),
        compiler_params=_params(1),
        name="attn_meta",
    )(sink, qm, k, vt, km, vm)
    return o_real, o_meta


def _pool_kernel(h_ref, prev_ref, next_ref, meta_ref, gain_ref, win_ref, wgrp_ref, scale_ref, wout_ref, o_ref,
                 hn_ref, u_ref, *run_refs, total_len):
    i = pl.program_id(1)
    last = pl.num_programs(1) - 1
    tm = h_ref.shape[0]
    n_sub = hn_ref.shape[0]
    sub = tm // n_sub
    span = sub + 2 * POOL_HALO
    gain = gain_ref[...]
    gdim = u_ref.shape[2] // len(POOL_WINDOWS)

    for s in range(n_sub):
        r0 = s * sub
        before = jnp.where(i == 0, meta_ref[...], prev_ref[...]) if s == 0 else h_ref[r0 - POOL_HALO:r0, :]
        hn_ref[s, 0:POOL_HALO, :] = _rms_norm(before, gain).astype(BF16)
        hn_ref[s, POOL_HALO:POOL_HALO + sub, :] = _rms_norm(h_ref[r0:r0 + sub, :], gain).astype(BF16)
        if s == n_sub - 1:
            after = jnp.where(i == last, 0.0, _rms_norm(next_ref[...], gain))
        else:
            after = _rms_norm(h_ref[r0 + sub:r0 + sub + POOL_HALO, :], gain)
        hn_ref[s, POOL_HALO + sub:, :] = after.astype(BF16)
        u_ref[s, 0:span, :] = _dot(hn_ref[s], win_ref[...])
        u_ref[s, span:, :] = jnp.zeros((u_ref.shape[1] - span, u_ref.shape[2]), F32)

    for s in range(n_sub):
        r0 = s * sub
        runs = {1: (u_ref, 0)}
        src_ref, src_col, width = u_ref, 0, 1
        for k, run_ref in enumerate(run_refs, start=1):
            col = k * gdim
            rows = span - F32_SUBLANES * k
            lo = F32_SUBLANES
            run_ref[s, lo:lo + rows, :] = (src_ref[s, lo:lo + rows, col - src_col:]
                                           + src_ref[s, lo + width:lo + width + rows, col - src_col:])
            src_ref, src_col, width = run_ref, col, 2 * width
            runs[width] = (run_ref, col)

        body = sub - POOL_HALO
        tail_pos = N_META + i * tm + r0 + body + lax.broadcasted_iota(jnp.int32, (POOL_HALO, 1), 0)
        mixed = []
        for g, window in enumerate(POOL_WINDOWS):
            half = window // 2
            run_ref, col = runs[half]
            cols = slice(g * gdim - col, (g + 1) * gdim - col)
            total = (run_ref[s, POOL_HALO - half:POOL_HALO - half + sub, cols]
                     + run_ref[s, POOL_HALO:POOL_HALO + sub, cols])
            count = (half + jnp.minimum(half, total_len - tail_pos)).astype(F32)
            mean = jnp.concatenate([total[:body] * (1.0 / window), total[body:] / count], axis=0)
            pooled = mean - u_ref[s, POOL_HALO:POOL_HALO + sub, g * gdim:(g + 1) * gdim]
            mixed.append(_dot(pooled.astype(BF16), wgrp_ref[g]))
        y = (jnp.concatenate(mixed, axis=1) * scale_ref[...]).astype(BF16)
        o_ref[r0:r0 + sub, :] = h_ref[r0:r0 + sub, :] + _dot(y, wout_ref[...])


def _pool_mixer(h, h_meta, gain, w_in, w_grp, scale, w_out):
    b, s, d = h.shape
    tm = min(POOL_TILE, s)
    per_tile = tm // POOL_HALO
    n_halo_blocks = s // POOL_HALO
    n_sub = max(tm // POOL_SUBTILE, 1)
    span = tm // n_sub + 2 * POOL_HALO
    n_groups = len(POOL_WINDOWS)
    gdim = d // n_groups
    assert POOL_WINDOWS == tuple(2 ** (g + 1) for g in range(n_groups)) and POOL_WINDOWS[-1] // 2 <= F32_SUBLANES
    return pl.pallas_call(
        functools.partial(_pool_kernel, total_len=N_META + s),
        out_shape=jax.ShapeDtypeStruct((b, s, d), h.dtype),
        grid=(b, s // tm),
        in_specs=[pl.BlockSpec((None, tm, d), lambda bi, i: (bi, i, 0)),
                  pl.BlockSpec((None, POOL_HALO, d), lambda bi, i: (bi, jnp.maximum(i * per_tile - 1, 0), 0)),
                  pl.BlockSpec((None, POOL_HALO, d),
                               lambda bi, i: (bi, jnp.minimum((i + 1) * per_tile, n_halo_blocks - 1), 0)),
                  pl.BlockSpec((None, N_META, d), lambda bi, i: (bi, 0, 0)),
                  _resident((1, d)), _resident(w_in.shape), _resident(w_grp.shape), _resident((1, d)),
                  _resident(w_out.shape)],
        out_specs=pl.BlockSpec((None, tm, d), lambda bi, i: (bi, i, 0)),
        scratch_shapes=[pltpu.VMEM((n_sub, span, d), BF16), pltpu.VMEM((n_sub, span + F32_SUBLANES, d), F32)]
                       + [pltpu.VMEM((n_sub, span, d - k * gdim), F32) for k in range(1, n_groups)],
        compiler_params=_params(2),
        name="pool_mixer",
    )(h, h, h, h_meta, gain, w_in, w_grp, scale, w_out)


def _permute_heads(w, n_heads, axis):
    n_pairs = n_heads // (GQA_GROUP * HEADS_PER_LANE_GROUP)
    split = w.shape[:axis] + (n_pairs, HEADS_PER_LANE_GROUP, GQA_GROUP, HEAD_DIM) + w.shape[axis + 1:]
    return jnp.swapaxes(w.reshape(split), axis + 1, axis + 2).reshape(w.shape)


def _segment_mean_matrix():
    seg = np.kron(np.eye(MXU_DIM // HEAD_DIM), np.ones((HEAD_DIM, HEAD_DIM))) / HEAD_DIM
    return jnp.asarray(seg, dtype=BF16)


def kernel(x, meta_tokens, ffn_norm, w_gate_up, w_down, mixer_norm, w_qkv, q_norm, k_norm, sink_logit, w_o,
           w_pool_in, w_pool_group, pool_scale, w_pool_out):
    b, s, d = x.shape
    depth = ffn_norm.shape[0]
    n_heads = sink_logit.shape[1]
    dq = n_heads * HEAD_DIM
    dkv = dq // GQA_GROUP
    assert depth == 2 and s % TOKEN_TILE == 0 and d == dq
    seg = _segment_mean_matrix()

    hr = x.reshape(b * s, d)
    hm = jnp.broadcast_to(meta_tokens[None].astype(x.dtype), (b, N_META, d)).reshape(b * N_META, d)

    def ffn_weights(layer, which):
        return (w_gate_up, (layer, which)), (w_down, (layer, which))

    def gain(layer, which):
        return ffn_norm[layer, which][None]

    w00 = w_gate_up[0, 0].astype(BF16), w_down[0, 0].astype(BF16)
    w_qkv_p = jnp.concatenate([_permute_heads(w_qkv[0][:, :dq], n_heads, axis=1), w_qkv[0][:, dq:]],
                              axis=1).astype(BF16)
    w_o_p = _permute_heads(w_o[0], n_heads, axis=0).astype(BF16)
    qkv_args = (mixer_norm[0][None], w_qkv_p, jnp.tile(q_norm[0], n_heads)[None],
                jnp.tile(k_norm[0], dkv // HEAD_DIM)[None], seg)

    hr, q, k, vt, *w01 = _stage(hr, gain(0, 0), *w00, qkv=qkv_args, cast=ffn_weights(0, 1))
    hm, qm, km, vmt = _stage(hm, gain(0, 0), *w00, qkv=qkv_args)
    vm = vmt.transpose(0, 2, 1).reshape(b, N_META, dkv)
    logit_bound = (HEAD_DIM ** 0.5 * LOG2E * ROUNDING_MARGIN) * jnp.max(jnp.abs(q_norm[0])) * jnp.max(jnp.abs(k_norm[0]))
    o_real, o_meta = _attention(sink_logit[0], logit_bound, q.reshape(b, s, dq), k.reshape(b, s, dkv), vt,
                                qm.reshape(b, N_META, dq), km.reshape(b, N_META, dkv), vm, vm.transpose(0, 2, 1))
    hr, *w10 = _stage(hr, gain(0, 1), *w01, att=o_real.reshape(b * s, dq), wo=w_o_p, cast=ffn_weights(1, 0))
    (hm,) = _stage(hm, gain(0, 1), *w01, att=o_meta.reshape(b * N_META, dq), wo=w_o_p)

    n_grp, gdim = w_pool_group.shape[1:3]
    pool_weights = ((w_pool_in, (0,)), (w_pool_group.reshape(1, n_grp * gdim, gdim), (0,)), (w_pool_out, (0,)))
    hr, *casts = _stage(hr, gain(1, 0), *w10, cast=ffn_weights(1, 1) + pool_weights)
    (hm,) = _stage(hm, gain(1, 0), *w10)
    w11, (w_in, w_grp, w_out) = casts[:2], casts[2:]
    hr = _pool_mixer(hr.reshape(b, s, d), hm.reshape(b, N_META, d), mixer_norm[1][None], w_in,
                     w_grp.reshape(n_grp, gdim, gdim), pool_scale[0][None], w_out)
    (hr,) = _stage(hr.reshape(b * s, d), gain(1, 1), *w11)
    return hr.reshape(b, s, d)
```

```python
import functools

import numpy as np
import jax
import jax.numpy as jnp
from jax import lax
from jax.experimental import pallas as pl
from jax.experimental.pallas import tpu as pltpu

F32 = jnp.float32
BF16 = jnp.bfloat16

N_META = 16
HEAD_DIM = 64
GQA_GROUP = 4
WINDOW = 128
BLOCK = 128
POOL_WINDOWS = (2, 4, 8, 16)
RMS_EPS = 1e-6
NEG_INF = -1e30
LOG2E = 1.4426950408889634

LANES = 128
F32_SUBLANES = 8
BF16_SUBLANES = 16
MXU_DIM = 256
HEADS_PER_LANE_GROUP = LANES // HEAD_DIM
BAND = 3 * BLOCK
POOL_HALO = 16

TOKEN_TILE = 1024
QKV_TILE = 1024
QKV_PIECE = 512
POOL_TILE = 1024
POOL_SUBTILE = 512
ATTN_BLOCKS_PER_STEP = 16
ATTN_BLOCK_GROUP = 8
SAFE_LOGIT_BOUND = 40.0
ROUNDING_MARGIN = 1.02
FF_CHUNK = 256
VMEM_LIMIT = 58 * 1024 * 1024


def _params(n_axes, vmem=VMEM_LIMIT):
    return pltpu.CompilerParams(dimension_semantics=("arbitrary",) * n_axes, vmem_limit_bytes=vmem)


def _resident(shape):
    nd = len(shape)
    return pl.BlockSpec(shape, lambda *_: (0,) * nd, pipeline_mode=pl.Buffered(1))


def _rows(tm, d):
    return pl.BlockSpec((tm, d), lambda i: (i, 0))


def _rms_norm(x, gain):
    return x * lax.rsqrt(jnp.mean(x * x, axis=-1, keepdims=True) + RMS_EPS) * gain


def _dot(a, b):
    return jnp.dot(a, b, preferred_element_type=F32)


def _dot_nt(a, b):
    return lax.dot_general(a, b, (((1,), (1,)), ((), ())), preferred_element_type=F32)


def _head_mean_square(t, seg_ref):
    sq = (t * t).astype(BF16)
    seg = seg_ref[...]
    cols = [_dot(sq[:, c * MXU_DIM:(c + 1) * MXU_DIM], seg) for c in range(t.shape[1] // MXU_DIM)]
    return cols[0] if len(cols) == 1 else jnp.concatenate(cols, axis=1)


def _project_qkv(h, gain_ref, w_ref, qgain_ref, kgain_ref, seg_ref, q_ref, k_ref, vt_ref):
    dq, dkv = q_ref.shape[1], k_ref.shape[1]
    piece = min(h.shape[0], QKV_PIECE)
    starts = range(0, h.shape[0], piece)
    qkvs = [_dot(_rms_norm(h[r:r + piece], gain_ref[...]).astype(BF16), w_ref[...]) for r in starts]
    for r, qkv in zip(starts, qkvs):
        q = qkv[:, :dq]
        k = qkv[:, dq:dq + dkv]
        v = qkv[:, dq + dkv:]
        q = q * lax.rsqrt(_head_mean_square(q, seg_ref) + RMS_EPS) * qgain_ref[...] * (HEAD_DIM ** -0.5 * LOG2E)
        k = k * lax.rsqrt(_head_mean_square(k, seg_ref) + RMS_EPS) * kgain_ref[...]
        q_ref[r:r + piece, :] = q.astype(q_ref.dtype)
        k_ref[r:r + piece, :] = k.astype(k_ref.dtype)
        for t in range(piece // BLOCK):
            vt_ref[r // BLOCK + t] = v[t * BLOCK:(t + 1) * BLOCK, :].T.astype(vt_ref.dtype)


def _stage_kernel(*refs, d_ff, chunk, pre_proj, post_qkv, n_cast):
    refs = list(refs)

    def take(n):
        taken, refs[:] = refs[:n], refs[n:]
        return taken

    (x_ref,) = take(1)
    att_ref, wo_ref = take(2) if pre_proj else (None, None)
    gain_ref, wgu_ref, wd_ref = take(3)
    qkv_in = take(5) if post_qkv else []
    cast_in = take(n_cast)
    (out_ref,) = take(1)
    qkv_out = take(3) if post_qkv else []
    cast_out = take(n_cast)
    (act_ref,) = take(1)

    x = x_ref[...]
    if pre_proj:
        x = x + _dot(att_ref[...], wo_ref[...])
    xn = _rms_norm(x, gain_ref[...]).astype(BF16)
    for c in range(d_ff // chunk):
        gate = _dot(xn, wgu_ref[:, c * chunk:(c + 1) * chunk])
        up = _dot(xn, wgu_ref[:, d_ff + c * chunk:d_ff + (c + 1) * chunk])
        act_ref[:, c * chunk:(c + 1) * chunk] = (jax.nn.silu(gate) * up).astype(BF16)
    h = x + 0.5 * _dot(act_ref[...], wd_ref[...])
    out_ref[...] = h
    if post_qkv:
        _project_qkv(h, *qkv_in, *qkv_out)
    for src, dst in zip(cast_in, cast_out):
        dst[...] = src[...].astype(dst.dtype)


def _cast_specs(rows, cols, prefix, steps):
    block = next(r for r in range(BF16_SUBLANES, rows + 1, BF16_SUBLANES) if rows % r == 0 and r * steps >= rows)
    last = rows // block - 1
    in_spec = pl.BlockSpec((None,) * len(prefix) + (block, cols), lambda i: (*prefix, jnp.minimum(i, last), 0))
    return in_spec, pl.BlockSpec((block, cols), lambda i: (jnp.minimum(i, last), 0))


def _stage(h, gain, wgu, wd, att=None, wo=None, qkv=None, cast=()):
    t, d = h.shape
    d_ff = wd.shape[0]
    tm = min(TOKEN_TILE if qkv is None else QKV_TILE, t)
    steps = t // tm
    operands, in_specs = [h], [_rows(tm, d)]
    if att is not None:
        operands += [att, wo]
        in_specs += [_rows(tm, att.shape[1]), _resident(wo.shape)]
    operands += [gain, wgu, wd]
    in_specs += [_resident(gain.shape), _resident(wgu.shape), _resident(wd.shape)]
    out_shape, out_specs = [jax.ShapeDtypeStruct((t, d), h.dtype)], [_rows(tm, d)]
    if qkv is not None:
        operands += list(qkv)
        in_specs += [_resident(a.shape) for a in qkv]
        dq, dkv = qkv[2].shape[1], qkv[3].shape[1]
        out_shape += [jax.ShapeDtypeStruct((t, dq), BF16), jax.ShapeDtypeStruct((t, dkv), BF16),
                      jax.ShapeDtypeStruct((t // BLOCK, dkv, BLOCK), BF16)]
        out_specs += [_rows(tm, dq), _rows(tm, dkv), pl.BlockSpec((tm // BLOCK, dkv, BLOCK), lambda i: (i, 0, 0))]
    for w, prefix in cast:
        in_spec, out_spec = _cast_specs(w.shape[-2], w.shape[-1], prefix, steps)
        operands.append(w)
        in_specs.append(in_spec)
        out_shape.append(jax.ShapeDtypeStruct(w.shape[-2:], BF16))
        out_specs.append(out_spec)
    return pl.pallas_call(
        functools.partial(_stage_kernel, d_ff=d_ff, chunk=FF_CHUNK, pre_proj=att is not None,
                          post_qkv=qkv is not None, n_cast=len(cast)),
        out_shape=out_shape,
        grid=(steps,),
        in_specs=in_specs,
        out_specs=out_specs,
        scratch_shapes=[pltpu.VMEM((tm, d_ff), BF16)],
        compiler_params=_params(1),
        name="stage",
    )(*operands)


def _slope2(head, n_heads):
    return 2.0 ** (-8.0 * (head + 1) / n_heads) * LOG2E


def _head_index(j, e, g):
    return (HEADS_PER_LANE_GROUP * j + e) * GQA_GROUP + g


def _stack_queries(q, j):
    rows = q.shape[0]
    low = lax.broadcasted_iota(jnp.int32, (rows, LANES), 1) < HEAD_DIM
    zero = jnp.zeros((rows, LANES), q.dtype)
    groups = [q[:, (j * GQA_GROUP + g) * LANES:(j * GQA_GROUP + g + 1) * LANES] for g in range(GQA_GROUP)]
    return jnp.concatenate([jnp.where(low, qg, zero) for qg in groups]
                           + [jnp.where(low, zero, qg) for qg in groups], axis=0)


def _attend_blocks(n0, last, row0, exact, sink_ref, shift_ref, q_ref, k_ref, vt_ref, km_ref, vmt_ref, o_ref,
                   bias_l_ref, bias_c_ref, bias_r_ref, bias_m_ref, vall_ref, p_ref):
    n_heads = q_ref.shape[1] // HEAD_DIM
    n_pairs = k_ref.shape[1] // LANES
    upper = lax.broadcasted_iota(jnp.int32, (LANES, BLOCK), 0) < HEAD_DIM
    low = lax.broadcasted_iota(jnp.int32, (BLOCK, LANES), 1) < HEAD_DIM
    zero = jnp.zeros((BLOCK, LANES), q_ref.dtype)
    heads_per_dot = MXU_DIM // BLOCK
    heads_per_pair = HEADS_PER_LANE_GROUP * GQA_GROUP

    inv = []
    for slot in range(p_ref.shape[0]):
        n = n0 + slot
        left = jnp.maximum(n - 1, 0)
        right = jnp.minimum(n + 1, last)
        no_left = (n == 0).astype(jnp.int32)
        no_right = (n == last).astype(jnp.int32)
        blocks = (left, n, right)
        keys = jnp.concatenate([k_ref[pl.ds(pl.multiple_of(blk * BLOCK, BLOCK), BLOCK), :] for blk in blocks]
                               + [km_ref[...]], axis=0)
        for i, blk in enumerate(blocks):
            vall_ref[slot, :, i * BLOCK:(i + 1) * BLOCK] = vt_ref[blk]
        vall_ref[slot, :, BAND:BAND + N_META] = vmt_ref[...]
        q = q_ref[pl.ds(row0 + slot * BLOCK, BLOCK), :]
        block_shift = (n * BLOCK).astype(F32)
        for j in range(n_pairs):
            kg = keys[:, j * LANES:(j + 1) * LANES]
            for e in range(HEADS_PER_LANE_GROUP):
                for g0 in range(0, GQA_GROUP, heads_per_dot):
                    groups = [q[:, (j * GQA_GROUP + g) * LANES:(j * GQA_GROUP + g + 1) * LANES]
                              for g in range(g0, g0 + heads_per_dot)]
                    halves = [jnp.where(low, qg, zero) if e == 0 else jnp.where(low, zero, qg) for qg in groups]
                    st = _dot_nt(kg, jnp.concatenate(halves, axis=0))
                    heads = [_head_index(j, e, g0 + gi) for gi in range(heads_per_dot)]
                    sinks = [sink_ref[head] * LOG2E - shift_ref[head] for head in heads]
                    pieces = (0, BLOCK, 2 * BLOCK, BAND, BAND + N_META)

                    def biased(gi, piece):
                        head = heads[gi]
                        bias = (bias_l_ref[no_left, head], bias_c_ref[head], bias_r_ref[no_right, head],
                                bias_m_ref[head] - _slope2(head, n_heads) * block_shift)[piece]
                        return st[pieces[piece]:pieces[piece + 1], gi * BLOCK:(gi + 1) * BLOCK] + bias

                    if exact:
                        parts = [[biased(gi, piece) for piece in range(4)] for gi in range(heads_per_dot)]
                        tops = []
                        for gi in range(heads_per_dot):
                            s_l, s_c, s_r, s_m = parts[gi]
                            m = jnp.max(jnp.maximum(jnp.maximum(s_l, s_c), s_r), axis=0, keepdims=True)
                            tops.append(jnp.maximum(jnp.maximum(m, jnp.max(s_m, axis=0, keepdims=True)), sinks[gi]))
                        denoms = [jnp.exp2(sinks[gi] - tops[gi]) for gi in range(heads_per_dot)]
                    else:
                        denoms = [jnp.exp2(jnp.full((1, BLOCK), sinks[gi], F32)) for gi in range(heads_per_dot)]
                    for piece in range(4):
                        for gi in range(heads_per_dot):
                            s = parts[gi][piece] - tops[gi] if exact else biased(gi, piece)
                            ex = jnp.exp2(s)
                            denoms[gi] = denoms[gi] + jnp.sum(ex, axis=0, keepdims=True)
                            cols = slice((e * GQA_GROUP + g0 + gi) * BLOCK, (e * GQA_GROUP + g0 + gi + 1) * BLOCK)
                            p_ref[slot, j, pieces[piece]:pieces[piece + 1], cols] = ex.astype(p_ref.dtype)
                    inv += [1.0 / denom for denom in denoms]

    for slot in range(p_ref.shape[0]):
        for j in range(n_pairs):
            out_t = _dot(vall_ref[slot, j * LANES:(j + 1) * LANES, :], p_ref[slot, j])
            base = (slot * n_pairs + j) * heads_per_pair
            for g in range(GQA_GROUP):
                c0, c1 = g * BLOCK, (GQA_GROUP + g) * BLOCK
                og_t = jnp.where(upper, out_t[:, c0:c0 + BLOCK] * inv[base + g],
                                 out_t[:, c1:c1 + BLOCK] * inv[base + GQA_GROUP + g])
                lanes = slice((j * GQA_GROUP + g) * LANES, (j * GQA_GROUP + g + 1) * LANES)
                o_ref[pl.ds(row0 + slot * BLOCK, BLOCK), lanes] = og_t.T.astype(o_ref.dtype)


def _attn_real_kernel(exact_ref, sink_ref, shift_ref, q_ref, k_ref, vt_ref, km_ref, vmt_ref, o_ref,
                      bias_l_ref, bias_c_ref, bias_r_ref, bias_m_ref, vall_ref, p_ref):
    first = (pl.program_id(0) == 0) & (pl.program_id(1) == 0)
    n_heads = q_ref.shape[1] // HEAD_DIM
    per_step = q_ref.shape[0] // BLOCK
    last = pl.num_programs(1) * per_step - 1

    @pl.when(first)
    def _init_tables():
        key = lax.broadcasted_iota(jnp.int32, (BLOCK, BLOCK), 0)
        qry = lax.broadcasted_iota(jnp.int32, (BLOCK, BLOCK), 1)
        masked = jnp.full((BLOCK, BLOCK), NEG_INF, F32)
        meta = lax.broadcasted_iota(jnp.int32, (N_META, BLOCK), 0)
        mqry = lax.broadcasted_iota(jnp.int32, (N_META, BLOCK), 1)
        for h in range(n_heads):
            slope = _slope2(h, n_heads)
            shift = shift_ref[h]
            for ref, delta in ((bias_l_ref, key - BLOCK - qry), (bias_c_ref, key - qry), (bias_r_ref, key + BLOCK - qry)):
                dist = jnp.abs(delta)
                table = jnp.where(dist <= WINDOW, -slope * dist.astype(F32) - shift, NEG_INF)
                if ref is bias_c_ref:
                    ref[h] = table
                else:
                    ref[0, h] = table
                    ref[1, h] = masked
            bias_m_ref[h] = -slope * (N_META + mqry - meta).astype(F32) - shift
        vall_ref[...] = jnp.zeros_like(vall_ref)
        p_ref[...] = jnp.zeros_like(p_ref)

    refs = (sink_ref, shift_ref, q_ref, k_ref, vt_ref, km_ref, vmt_ref, o_ref,
            bias_l_ref, bias_c_ref, bias_r_ref, bias_m_ref, vall_ref, p_ref)
    needs_max = exact_ref[0] != 0

    group = p_ref.shape[0]

    def blocks(it, carry):
        n0 = pl.program_id(1) * per_step + it * group
        row0 = pl.multiple_of(it * (group * BLOCK), BLOCK)

        @pl.when(needs_max)
        def _exact():
            _attend_blocks(n0, last, row0, True, *refs)

        @pl.when(jnp.logical_not(needs_max))
        def _bounded():
            _attend_blocks(n0, last, row0, False, *refs)

        return carry

    lax.fori_loop(0, per_step // group, blocks, 0)


def _attn_meta_kernel(sink_ref, q_ref, k_ref, vt_ref, km_ref, vm_ref, o_ref):
    rows = q_ref.shape[0]
    n_heads = q_ref.shape[1] // HEAD_DIM
    keys = jnp.concatenate([k_ref[...], km_ref[...]], axis=0)
    vals = jnp.concatenate([vt_ref[0].astype(F32).T.astype(BF16), vm_ref[...]], axis=0)
    nk = BLOCK + N_META
    qpos = lax.broadcasted_iota(jnp.int32, (rows, nk), 0)
    col = lax.broadcasted_iota(jnp.int32, (rows, nk), 1)
    idist = jnp.abs(qpos - jnp.where(col < BLOCK, N_META + col, col - BLOCK))
    valid = idist <= WINDOW
    dist = idist.astype(F32)
    low = lax.broadcasted_iota(jnp.int32, (rows, LANES), 1) < HEAD_DIM
    q = q_ref[...]
    for j in range(keys.shape[1] // LANES):
        scores = _dot_nt(_stack_queries(q, j), keys[:, j * LANES:(j + 1) * LANES])
        probs = []
        for e in range(HEADS_PER_LANE_GROUP):
            for g in range(GQA_GROUP):
                head = _head_index(j, e, g)
                r0 = (e * GQA_GROUP + g) * rows
                s = jnp.where(valid, scores[r0:r0 + rows] - _slope2(head, n_heads) * dist, NEG_INF)
                sink = sink_ref[head] * LOG2E
                m = jnp.maximum(jnp.max(s, axis=-1, keepdims=True), sink)
                ex = jnp.exp2(s - m)
                denom = jnp.sum(ex, axis=-1, keepdims=True) + jnp.exp2(sink - m)
                probs.append((ex * (1.0 / denom)).astype(BF16))
        out = _dot(jnp.concatenate(probs, axis=0), vals[:, j * LANES:(j + 1) * LANES])
        for g in range(GQA_GROUP):
            og = jnp.where(low, out[g * rows:(g + 1) * rows], out[(GQA_GROUP + g) * rows:(GQA_GROUP + g + 1) * rows])
            lanes = slice((j * GQA_GROUP + g) * LANES, (j * GQA_GROUP + g + 1) * LANES)
            o_ref[:, lanes] = og.astype(o_ref.dtype)


def _attention(sink, logit_bound, q, k, vt, qm, km, vm, vmt):
    b, s, dq = q.shape
    dkv = k.shape[2]
    nb = s // BLOCK
    n_heads = dq // HEAD_DIM
    n_pairs = dkv // LANES
    padded_keys = 2 * MXU_DIM
    rows = min(ATTN_BLOCKS_PER_STEP * BLOCK, s)
    exact = logit_bound > SAFE_LOGIT_BOUND
    shift = jnp.where(exact, 0.0, jnp.maximum(logit_bound, sink * LOG2E))
    smem = pl.BlockSpec(memory_space=pltpu.SMEM)
    o_real = pl.pallas_call(
        _attn_real_kernel,
        out_shape=jax.ShapeDtypeStruct((b, s, dq), BF16),
        grid=(b, s // rows),
        in_specs=[smem, smem, smem,
                  pl.BlockSpec((None, rows, dq), lambda i, n: (i, n, 0)),
                  pl.BlockSpec((None, s, dkv), lambda i, n: (i, 0, 0)),
                  pl.BlockSpec((nb, dkv, BLOCK), lambda i, n: (i, 0, 0)),
                  pl.BlockSpec((None, N_META, dkv), lambda i, n: (i, 0, 0)),
                  pl.BlockSpec((None, dkv, N_META), lambda i, n: (i, 0, 0))],
        out_specs=pl.BlockSpec((None, rows, dq), lambda i, n: (i, n, 0)),
        scratch_shapes=[pltpu.VMEM((2, n_heads, BLOCK, BLOCK), F32), pltpu.VMEM((n_heads, BLOCK, BLOCK), F32),
                        pltpu.VMEM((2, n_heads, BLOCK, BLOCK), F32), pltpu.VMEM((n_heads, N_META, BLOCK), F32),
                        pltpu.VMEM((ATTN_BLOCK_GROUP, dkv, padded_keys), BF16),
                        pltpu.VMEM((ATTN_BLOCK_GROUP, n_pairs, padded_keys, 2 * GQA_GROUP * BLOCK), BF16)],
        compiler_params=_params(2),
        name="attn_real",
    )(exact.astype(jnp.int32)[None], sink, shift.astype(F32), q, k, vt, km, vmt)
    o_meta = pl.pallas_call(
        _attn_meta_kernel,
        out_shape=jax.ShapeDtypeStruct((b, N_META, dq), BF16),
        grid=(b,),
        in_specs=[smem, pl.BlockSpec((None, N_META, dq), lambda i: (i, 0, 0)),
                  pl.BlockSpec((None, BLOCK, dkv), lambda i: (i, 0, 0)),
                  pl.BlockSpec((1, dkv, BLOCK), lambda i: (i * nb, 0, 0)),
                  pl.BlockSpec((None, N_META, dkv), lambda i: (i, 0, 0)),
                  pl.BlockSpec((None, N_META, dkv), lambda i: (i, 0, 0))],
        out_specs=pl.BlockSpec((None, N_META, dq), lambda i: (i, 0, 0)),
        compiler_params=_params(1),
        name="attn_meta",
    )(sink, qm, k, vt, km, vm)
    return o_real, o_meta


def _pool_kernel(h_ref, prev_ref, next_ref, meta_ref, gain_ref, win_ref, wgrp_ref, scale_ref, wout_ref, o_ref,
                 hn_ref, u_ref, *run_refs, total_len):
    i = pl.program_id(1)
    last = pl.num_programs(1) - 1
    tm = h_ref.shape[0]
    n_sub = hn_ref.shape[0]
    sub = tm // n_sub
    span = sub + 2 * POOL_HALO
    gain = gain_ref[...]
    gdim = u_ref.shape[2] // len(POOL_WINDOWS)

    for s in range(n_sub):
        r0 = s * sub
        before = jnp.where(i == 0, meta_ref[...], prev_ref[...]) if s == 0 else h_ref[r0 - POOL_HALO:r0, :]
        hn_ref[s, 0:POOL_HALO, :] = _rms_norm(before, gain).astype(BF16)
        hn_ref[s, POOL_HALO:POOL_HALO + sub, :] = _rms_norm(h_ref[r0:r0 + sub, :], gain).astype(BF16)
        if s == n_sub - 1:
            after = jnp.where(i == last, 0.0, _rms_norm(next_ref[...], gain))
        else:
            after = _rms_norm(h_ref[r0 + sub:r0 + sub + POOL_HALO, :], gain)
        hn_ref[s, POOL_HALO + sub:, :] = after.astype(BF16)
        u_ref[s, 0:span, :] = _dot(hn_ref[s], win_ref[...])
        u_ref[s, span:, :] = jnp.zeros((u_ref.shape[1] - span, u_ref.shape[2]), F32)

    for s in range(n_sub):
        r0 = s * sub
        runs = {1: (u_ref, 0)}
        src_ref, src_col, width = u_ref, 0, 1
        for k, run_ref in enumerate(run_refs, start=1):
            col = k * gdim
            rows = span - F32_SUBLANES * k
            lo = F32_SUBLANES
            run_ref[s, lo:lo + rows, :] = (src_ref[s, lo:lo + rows, col - src_col:]
                                           + src_ref[s, lo + width:lo + width + rows, col - src_col:])
            src_ref, src_col, width = run_ref, col, 2 * width
            runs[width] = (run_ref, col)

        body = sub - POOL_HALO
        tail_pos = N_META + i * tm + r0 + body + lax.broadcasted_iota(jnp.int32, (POOL_HALO, 1), 0)
        mixed = []
        for g, window in enumerate(POOL_WINDOWS):
            half = window // 2
            run_ref, col = runs[half]
            cols = slice(g * gdim - col, (g + 1) * gdim - col)
            total = (run_ref[s, POOL_HALO - half:POOL_HALO - half + sub, cols]
                     + run_ref[s, POOL_HALO:POOL_HALO + sub, cols])
            count = (half + jnp.minimum(half, total_len - tail_pos)).astype(F32)
            mean = jnp.concatenate([total[:body] * (1.0 / window), total[body:] / count], axis=0)
            pooled = mean - u_ref[s, POOL_HALO:POOL_HALO + sub, g * gdim:(g + 1) * gdim]
            mixed.append(_dot(pooled.astype(BF16), wgrp_ref[g]))
        y = (jnp.concatenate(mixed, axis=1) * scale_ref[...]).astype(BF16)
        o_ref[r0:r0 + sub, :] = h_ref[r0:r0 + sub, :] + _dot(y, wout_ref[...])


def _pool_mixer(h, h_meta, gain, w_in, w_grp, scale, w_out):
    b, s, d = h.shape
    tm = min(POOL_TILE, s)
    per_tile = tm // POOL_HALO
    n_halo_blocks = s // POOL_HALO
    n_sub = max(tm // POOL_SUBTILE, 1)
    span = tm // n_sub + 2 * POOL_HALO
    n_groups = len(POOL_WINDOWS)
    gdim = d // n_groups
    assert POOL_WINDOWS == tuple(2 ** (g + 1) for g in range(n_groups)) and POOL_WINDOWS[-1] // 2 <= F32_SUBLANES
    return pl.pallas_call(
        functools.partial(_pool_kernel, total_len=N_META + s),
        out_shape=jax.ShapeDtypeStruct((b, s, d), h.dtype),
        grid=(b, s // tm),
        in_specs=[pl.BlockSpec((None, tm, d), lambda bi, i: (bi, i, 0)),
                  pl.BlockSpec((None, POOL_HALO, d), lambda bi, i: (bi, jnp.maximum(i * per_tile - 1, 0), 0)),
                  pl.BlockSpec((None, POOL_HALO, d),
                               lambda bi, i: (bi, jnp.minimum((i + 1) * per_tile, n_halo_blocks - 1), 0)),
                  pl.BlockSpec((None, N_META, d), lambda bi, i: (bi, 0, 0)),
                  _resident((1, d)), _resident(w_in.shape), _resident(w_grp.shape), _resident((1, d)),
                  _resident(w_out.shape)],
        out_specs=pl.BlockSpec((None, tm, d), lambda bi, i: (bi, i, 0)),
        scratch_shapes=[pltpu.VMEM((n_sub, span, d), BF16), pltpu.VMEM((n_sub, span + F32_SUBLANES, d), F32)]
                       + [pltpu.VMEM((n_sub, span, d - k * gdim), F32) for k in range(1, n_groups)],
        compiler_params=_params(2),
        name="pool_mixer",
    )(h, h, h, h_meta, gain, w_in, w_grp, scale, w_out)


def _permute_heads(w, n_heads, axis):
    n_pairs = n_heads // (GQA_GROUP * HEADS_PER_LANE_GROUP)
    split = w.shape[:axis] + (n_pairs, HEADS_PER_LANE_GROUP, GQA_GROUP, HEAD_DIM) + w.shape[axis + 1:]
    return jnp.swapaxes(w.reshape(split), axis + 1, axis + 2).reshape(w.shape)


def _segment_mean_matrix():
    seg = np.kron(np.eye(MXU_DIM // HEAD_DIM), np.ones((HEAD_DIM, HEAD_DIM))) / HEAD_DIM
    return jnp.asarray(seg, dtype=BF16)


def kernel(x, meta_tokens, ffn_norm, w_gate_up, w_down, mixer_norm, w_qkv, q_norm, k_norm, sink_logit, w_o,
           w_pool_in, w_pool_group, pool_scale, w_pool_out):
    b, s, d = x.shape
    depth = ffn_norm.shape[0]
    n_heads = sink_logit.shape[1]
    dq = n_heads * HEAD_DIM
    dkv = dq // GQA_GROUP
    assert depth == 2 and s % TOKEN_TILE == 0 and d == dq
    seg = _segment_mean_matrix()

    hr = x.reshape(b * s, d)
    hm = jnp.broadcast_to(meta_tokens[None].astype(x.dtype), (b, N_META, d)).reshape(b * N_META, d)

    def ffn_weights(layer, which):
        return (w_gate_up, (layer, which)), (w_down, (layer, which))

    def gain(layer, which):
        return ffn_norm[layer, which][None]

    w00 = w_gate_up[0, 0].astype(BF16), w_down[0, 0].astype(BF16)
    w_qkv_p = jnp.concatenate([_permute_heads(w_qkv[0][:, :dq], n_heads, axis=1), w_qkv[0][:, dq:]],
                              axis=1).astype(BF16)
    w_o_p = _permute_heads(w_o[0], n_heads, axis=0).astype(BF16)
    qkv_args = (mixer_norm[0][None], w_qkv_p, jnp.tile(q_norm[0], n_heads)[None],
                jnp.tile(k_norm[0], dkv // HEAD_DIM)[None], seg)

    hr, q, k, vt, *w01 = _stage(hr, gain(0, 0), *w00, qkv=qkv_args, cast=ffn_weights(0, 1))
    hm, qm, km, vmt = _stage(hm, gain(0, 0), *w00, qkv=qkv_args)
    vm = vmt.transpose(0, 2, 1).reshape(b, N_META, dkv)
    logit_bound = (HEAD_DIM ** 0.5 * LOG2E * ROUNDING_MARGIN) * jnp.max(jnp.abs(q_norm[0])) * jnp.max(jnp.abs(k_norm[0]))
    o_real, o_meta = _attention(sink_logit[0], logit_bound, q.reshape(b, s, dq), k.reshape(b, s, dkv), vt,
                                qm.reshape(b, N_META, dq), km.reshape(b, N_META, dkv), vm, vm.transpose(0, 2, 1))
    hr, *w10 = _stage(hr, gain(0, 1), *w01, att=o_real.reshape(b * s, dq), wo=w_o_p, cast=ffn_weights(1, 0))
    (hm,) = _stage(hm, gain(0, 1), *w01, att=o_meta.reshape(b * N_META, dq), wo=w_o_p)

    n_grp, gdim = w_pool_group.shape[1:3]
    pool_weights = ((w_pool_in, (0,)), (w_pool_group.reshape(1, n_grp * gdim, gdim), (0,)), (w_pool_out, (0,)))
    hr, *casts = _stage(hr, gain(1, 0), *w10, cast=ffn_weights(1, 1) + pool_weights)
    (hm,) = _stage(hm, gain(1, 0), *w10)
    w11, (w_in, w_grp, w_out) = casts[:2], casts[2:]
    hr = _pool_mixer(hr.reshape(b, s, d), hm.reshape(b, N_META, d), mixer_norm[1][None], w_in,
                     w_grp.reshape(n_grp, gdim, gdim), pool_scale[0][None], w_out)
    (hr,) = _stage(hr.reshape(b * s, d), gain(1, 1), *w11)
    return hr.reshape(b, s, d)
```
